```python
import jax, jax.numpy as jnp
from jax import lax
import numpy as np

D_MODEL = 1024
BATCH = 4
SEQ = 8192
DEPTH = 4

N_MEM = 256
D_FF = 2816
EPS = 1e-6
GLA_HEADS = 4
GLA_DK = D_MODEL // 2 // GLA_HEADS
GLA_DV = D_MODEL // GLA_HEADS
GLA_GATE_RANK = 16
GLA_GATE_TAU = 16.0
GLA_CHUNK = 64
MLA_HEADS = 8
MLA_Q_RANK = 384
MLA_KV_RANK = 256
MLA_NOPE = 128
MLA_ROPE = 64
MLA_DV = 128
ROPE_BASE = 10000.0
Q_BLOCK = 128
MEM_HEADS = 4
MEM_DH = 128
MEM_Q_WIDTH = MEM_HEADS * MEM_DH
MIX_WIDTH = D_MODEL + MEM_Q_WIDTH
GLA_WIDTHS = (GLA_HEADS * GLA_DK, GLA_HEADS * GLA_DK, GLA_HEADS * GLA_DV,
              GLA_GATE_RANK, GLA_HEADS * GLA_DV, MEM_Q_WIDTH)
GLA_IN = sum(GLA_WIDTHS)
GLA_SPLITS = tuple(int(c) for c in np.cumsum(GLA_WIDTHS)[:-1])
MLA_WIDTHS = (MLA_Q_RANK, MLA_KV_RANK, MLA_ROPE, MEM_Q_WIDTH)
MLA_IN = sum(MLA_WIDTHS)
MLA_SPLITS = tuple(int(c) for c in np.cumsum(MLA_WIDTHS)[:-1])
N_GLA = (DEPTH + 1) // 2
N_MLA = DEPTH // 2

kernel_name = "hybrid_gla_mla_macaron_memory_trunk"


def rmsnorm(x, g):
    xf = x.astype(jnp.float32)
    y = xf * lax.rsqrt(jnp.mean(xf * xf, axis=-1, keepdims=True) + EPS)
    return (y * g.astype(jnp.float32)).astype(x.dtype)


def swiglu(h, w_gate, w_up, w_down):
    return (jax.nn.silu(h @ w_gate) * (h @ w_up)) @ w_down


def rope_tables(seq, dim):
    inv = 1.0 / (ROPE_BASE ** (jnp.arange(0, dim, 2, dtype=jnp.float32) / dim))
    ang = jnp.arange(seq, dtype=jnp.float32)[:, None] * inv[None, :]
    return jnp.cos(ang)[:, None, :], jnp.sin(ang)[:, None, :]


def apply_rope(t, cos, sin):
    t1, t2 = jnp.split(t, 2, axis=-1)
    c = cos.astype(t.dtype)
    s = sin.astype(t.dtype)
    return jnp.concatenate([t1 * c - t2 * s, t1 * s + t2 * c], axis=-1)


def gla_attention(q, k, v, log_a):
    B, S, H, DK = q.shape
    DV = v.shape[-1]
    C = GLA_CHUNK
    N = S // C
    f32 = jnp.float32

    def to_chunks(t):
        return t.reshape(B, N, C, H, t.shape[-1]).transpose(0, 1, 3, 2, 4)

    qc = to_chunks(q.astype(f32)) * (DK ** -0.5)
    kc = to_chunks(k.astype(f32))
    vc = to_chunks(v.astype(f32))
    bc = jnp.cumsum(to_chunks(log_a.astype(f32)), axis=3)
    b_last = bc[:, :, :, C - 1:C, :]
    b_ref = bc[:, :, :, C // 2 - 1:C // 2, :]

    scores = jnp.einsum('bnhid,bnhjd->bnhij', qc * jnp.exp(bc - b_ref), kc * jnp.exp(b_ref - bc))
    causal = jnp.tril(jnp.ones((C, C), dtype=bool))
    scores = jnp.where(causal, scores, 0.0)
    o_intra = jnp.einsum('bnhij,bnhjv->bnhiv', scores, vc)

    u = jnp.einsum('bnhjd,bnhjv->bnhdv', kc * jnp.exp(b_last - bc), vc)
    decay = jnp.exp(b_last[:, :, :, 0, :])

    def step(s, inp):
        d_n, u_n = inp
        return d_n[..., None] * s + u_n, s

    s0 = jnp.zeros((B, H, DK, DV), f32)
    _, s_prev = lax.scan(step, s0, (decay.transpose(1, 0, 2, 3), u.transpose(1, 0, 2, 3, 4)))
    s_prev = s_prev.transpose(1, 0, 2, 3, 4)
    o_inter = jnp.einsum('bnhid,bnhdv->bnhiv', qc * jnp.exp(bc), s_prev)
    o = (o_intra + o_inter).transpose(0, 1, 3, 2, 4).reshape(B, S, H, DV)
    return o.astype(v.dtype)


def causal_block_attention(q, k, v, scale):
    B, H, S, _ = q.shape
    nb = S // Q_BLOCK
    qb = q.reshape(B, H, nb, Q_BLOCK, q.shape[-1]).transpose(2, 0, 1, 3, 4)
    kpos = jnp.arange(S)

    def one_block(args):
        i, q_i = args
        s = jnp.einsum('bhqd,bhkd->bhqk', q_i, k).astype(jnp.float32) * scale
        qpos = i * Q_BLOCK + jnp.arange(Q_BLOCK)
        s = jnp.where(kpos[None, :] <= qpos[:, None], s, -jnp.inf)
        p = jax.nn.softmax(s, axis=-1).astype(v.dtype)
        return jnp.einsum('bhqk,bhkd->bhqd', p, v)

    o = lax.map(one_block, (jnp.arange(nb), qb))
    return o.transpose(1, 2, 0, 3, 4).reshape(B, H, S, v.shape[-1])


def mla_attention(c_q, c_kv, k_r, q_norm, kv_norm, w_uq, w_ukv,
                  qn_norm, qr_norm, kn_norm, kr_norm, cos, sin):
    B, S, _ = c_q.shape
    q = (rmsnorm(c_q, q_norm) @ w_uq).reshape(B, S, MLA_HEADS, MLA_NOPE + MLA_ROPE)
    kv = (rmsnorm(c_kv, kv_norm) @ w_ukv).reshape(B, S, MLA_HEADS, MLA_NOPE + MLA_DV)
    q_nope = rmsnorm(q[..., :MLA_NOPE], qn_norm)
    q_rope = apply_rope(rmsnorm(q[..., MLA_NOPE:], qr_norm), cos, sin)
    k_nope = rmsnorm(kv[..., :MLA_NOPE], kn_norm)
    v = kv[..., MLA_NOPE:]
    k_rope = apply_rope(rmsnorm(k_r, kr_norm)[:, :, None, :], cos, sin)
    k_rope = jnp.broadcast_to(k_rope, (B, S, MLA_HEADS, MLA_ROPE))
    qh = jnp.concatenate([q_nope, q_rope], axis=-1).transpose(0, 2, 1, 3)
    kh = jnp.concatenate([k_nope, k_rope], axis=-1).transpose(0, 2, 1, 3)
    vh = v.transpose(0, 2, 1, 3)
    o = causal_block_attention(qh, kh, vh, (MLA_NOPE + MLA_ROPE) ** -0.5)
    return o.transpose(0, 2, 1, 3).reshape(B, S, MLA_HEADS * MLA_DV)


def memory_cross_attention(q_mem, mem_h, w_kv, q_gain, k_gain):
    B, S, _ = q_mem.shape
    M = mem_h.shape[1]
    k, v = jnp.split(mem_h @ w_kv, 2, axis=-1)
    q = rmsnorm(q_mem.reshape(B, S, MEM_HEADS, MEM_DH), q_gain)
    k = rmsnorm(k.reshape(B, M, MEM_HEADS, MEM_DH), k_gain)
    v = v.reshape(B, M, MEM_HEADS, MEM_DH)
    s = jnp.einsum('bshd,bmhd->bhsm', q, k).astype(jnp.float32) * (MEM_DH ** -0.5)
    p = jax.nn.softmax(s, axis=-1).astype(v.dtype)
    return jnp.einsum('bhsm,bmhd->bshd', p, v).reshape(B, S, MEM_HEADS * MEM_DH)


def setup_inputs(seed: int = 0) -> dict:
    key = jax.random.key(seed)
    ks = iter(jax.random.split(key, 48))

    def w(shape, fan_in):
        return jax.random.normal(next(ks), shape, jnp.float32) * (fan_in ** -0.5)

    def gain(shape):
        return 1.0 + 0.02 * jax.random.normal(next(ks), shape, jnp.float32)

    def bias(shape, scale):
        return scale * jax.random.normal(next(ks), shape, jnp.float32)

    return {
        "x": jax.random.normal(next(ks), (BATCH, SEQ, D_MODEL), jnp.float32),
        "mem": jax.random.normal(next(ks), (BATCH, N_MEM, D_MODEL), jnp.float32),
        "ffn1_norm": gain((DEPTH, D_MODEL)),
        "ffn1_w_gate": w((DEPTH, D_MODEL, D_FF), D_MODEL),
        "ffn1_w_up": w((DEPTH, D_MODEL, D_FF), D_MODEL),
        "ffn1_w_down": w((DEPTH, D_FF, D_MODEL), D_FF),
        "ffn2_norm": gain((DEPTH, D_MODEL)),
        "ffn2_w_gate": w((DEPTH, D_MODEL, D_FF), D_MODEL),
        "ffn2_w_up": w((DEPTH, D_MODEL, D_FF), D_MODEL),
        "ffn2_w_down": w((DEPTH, D_FF, D_MODEL), D_FF),
        "mix_norm": gain((DEPTH, D_MODEL)),
        "w_out": w((DEPTH, MIX_WIDTH, D_MODEL), MIX_WIDTH),
        "mem_norm": gain((DEPTH, D_MODEL)),
        "mem_w_kv": w((DEPTH, D_MODEL, 2 * MEM_Q_WIDTH), D_MODEL),
        "memq_norm": gain((DEPTH, MEM_DH)),
        "memk_norm": gain((DEPTH, MEM_DH)),
        "gla_w_in": w((N_GLA, D_MODEL, GLA_IN), D_MODEL),
        "gla_w_alpha": w((N_GLA, GLA_GATE_RANK, GLA_HEADS * GLA_DK), GLA_GATE_RANK),
        "gla_b_alpha": bias((N_GLA, GLA_HEADS * GLA_DK), 0.1),
        "gla_out_norm": gain((N_GLA, GLA_DV)),
        "mla_w_in": w((N_MLA, D_MODEL, MLA_IN), D_MODEL),
        "mla_q_norm": gain((N_MLA, MLA_Q_RANK)),
        "mla_kv_norm": gain((N_MLA, MLA_KV_RANK)),
        "mla_w_uq": w((N_MLA, MLA_Q_RANK, MLA_HEADS * (MLA_NOPE + MLA_ROPE)), MLA_Q_RANK),
        "mla_w_ukv": w((N_MLA, MLA_KV_RANK, MLA_HEADS * (MLA_NOPE + MLA_DV)), MLA_KV_RANK),
        "mla_qn_norm": gain((N_MLA, MLA_NOPE)),
        "mla_qr_norm": gain((N_MLA, MLA_ROPE)),
        "mla_kn_norm": gain((N_MLA, MLA_NOPE)),
        "mla_kr_norm": gain((N_MLA, MLA_ROPE)),
    }


def reference(x, mem, ffn1_norm, ffn1_w_gate, ffn1_w_up, ffn1_w_down,
              ffn2_norm, ffn2_w_gate, ffn2_w_up, ffn2_w_down,
              mix_norm, w_out, mem_norm, mem_w_kv, memq_norm, memk_norm,
              gla_w_in, gla_w_alpha, gla_b_alpha, gla_out_norm,
              mla_w_in, mla_q_norm, mla_kv_norm, mla_w_uq, mla_w_ukv,
              mla_qn_norm, mla_qr_norm, mla_kn_norm, mla_kr_norm):
    B, S, _ = x.shape
    cos, sin = rope_tables(S, MLA_ROPE)
    for i in range(DEPTH):
        x = x + 0.5 * swiglu(rmsnorm(x, ffn1_norm[i]), ffn1_w_gate[i], ffn1_w_up[i], ffn1_w_down[i])
        h = rmsnorm(x, mix_norm[i])
        j = i // 2
        if i % 2 == 0:
            q, k, v, a_low, r, q_mem = jnp.split(h @ gla_w_in[j], GLA_SPLITS, axis=-1)
            log_a = jax.nn.log_sigmoid(a_low @ gla_w_alpha[j] + gla_b_alpha[j]) / GLA_GATE_TAU
            o = gla_attention(q.reshape(B, S, GLA_HEADS, GLA_DK),
                              k.reshape(B, S, GLA_HEADS, GLA_DK),
                              v.reshape(B, S, GLA_HEADS, GLA_DV),
                              log_a.reshape(B, S, GLA_HEADS, GLA_DK))
            o = rmsnorm(o, gla_out_norm[j]).reshape(B, S, GLA_HEADS * GLA_DV) * jax.nn.silu(r)
        else:
            c_q, c_kv, k_r, q_mem = jnp.split(h @ mla_w_in[j], MLA_SPLITS, axis=-1)
            o = mla_attention(c_q, c_kv, k_r, mla_q_norm[j], mla_kv_norm[j], mla_w_uq[j], mla_w_ukv[j],
                              mla_qn_norm[j], mla_qr_norm[j], mla_kn_norm[j], mla_kr_norm[j], cos, sin)
        o_mem = memory_cross_attention(q_mem, rmsnorm(mem, mem_norm[i]), mem_w_kv[i],
                                       memq_norm[i], memk_norm[i])
        x = x + jnp.concatenate([o, o_mem], axis=-1) @ w_out[i]
        x = x + 0.5 * swiglu(rmsnorm(x, ffn2_norm[i]), ffn2_w_gate[i], ffn2_w_up[i], ffn2_w_down[i])
    return x
```

```python
import functools

import jax
import jax.numpy as jnp
from jax import lax
from jax.experimental import pallas as pl
from jax.experimental.pallas import tpu as pltpu

F32 = jnp.float32
BF16 = jnp.bfloat16

EPS = 1e-6
GLA_HEADS = 4
GLA_GATE_RANK = 16
GLA_GATE_TAU = 16.0
GLA_CHUNK = 64
MLA_HEADS = 8
MLA_Q_RANK = 384
MLA_KV_RANK = 256
MLA_NOPE = 128
MLA_ROPE = 64
MLA_DV = 128
ROPE_BASE = 10000.0
MEM_HEADS = 4
MEM_DH = 128

LANES = 128
MLA_QK_PAD = 2 * LANES
VMEM_LIMIT_BYTES = 56 * 1024 * 1024

TOKEN_TILE = 512
FLASH_TQ = 512
FLASH_TK = 512
GLA_ROWS = 512

NEG_BIG = -1e30


def _params(*sem):
    return pltpu.CompilerParams(dimension_semantics=sem, vmem_limit_bytes=VMEM_LIMIT_BYTES)


def _const_spec(shape):
    nd = len(shape)
    return pl.BlockSpec(shape, lambda *_: (0,) * nd, pipeline_mode=pl.Buffered(1))


def _rms(x, gain, denom=None):
    d = x.shape[-1] if denom is None else denom
    ms = jnp.sum(x * x, axis=-1, keepdims=True) * (1.0 / d)
    return x * lax.rsqrt(ms + EPS) * gain


def _dot(a, b):
    return jnp.dot(a, b, preferred_element_type=F32)


def _dot_nt(a, b):
    return lax.dot_general(a, b, (((1,), (1,)), ((), ())), preferred_element_type=F32)


def _dot_tn(a, b):
    return lax.dot_general(a, b, (((0,), (0,)), ((), ())), preferred_element_type=F32)


def _ffn_body(x_ref, g_ref, wg_ref, wu_ref, wd_ref, o_ref):
    x = x_ref[...]
    h = _rms(x, g_ref[...]).astype(BF16)
    gate = _dot(h, wg_ref[...])
    up = _dot(h, wu_ref[...])
    a = (gate * jax.nn.sigmoid(gate) * up).astype(BF16)
    o_ref[...] = x + 0.5 * _dot(a, wd_ref[...])


def _ffn(x, gain, w_gate, w_up, w_down):
    t, d = x.shape
    f = w_gate.shape[1]
    tm = min(TOKEN_TILE, t)
    row = pl.BlockSpec((tm, d), lambda i: (i, 0))
    return pl.pallas_call(
        _ffn_body,
        grid=(t // tm,),
        in_specs=[row, _const_spec((1, d)), _const_spec((d, f)), _const_spec((d, f)), _const_spec((f, d))],
        out_specs=row,
        out_shape=jax.ShapeDtypeStruct((t, d), F32),
        compiler_params=_params("parallel"),
        name="ffn",
    )(x, gain, w_gate, w_up, w_down)


def _mem_q(p, gain):
    heads = [_rms(p[:, h * MEM_DH:(h + 1) * MEM_DH], gain) * (MEM_DH ** -0.5) for h in range(MEM_HEADS)]
    return jnp.concatenate(heads, axis=-1).astype(BF16)


def _gla_in_body(x_ref, g_ref, w_ref, wa_ref, ba_ref, mqg_ref,
                 q_ref, k_ref, v_ref, r_ref, qm_ref, la_ref, *, dqk, dv, dmem):
    h = _rms(x_ref[...], g_ref[...]).astype(BF16)
    p = _dot(h, w_ref[...])
    o = 0
    q_ref[...] = p[:, o:o + dqk].astype(BF16); o += dqk
    k_ref[...] = p[:, o:o + dqk].astype(BF16); o += dqk
    v_ref[...] = p[:, o:o + dv].astype(BF16); o += dv
    r_ref[...] = p[:, o:o + dv].astype(BF16); o += dv
    qm_ref[...] = _mem_q(p[:, o:o + dmem], mqg_ref[...]); o += dmem
    a_low = p[:, o:o + LANES].astype(BF16)
    z = _dot(a_low, wa_ref[...]) + ba_ref[...]
    log_sig = jnp.minimum(z, 0.0) - jnp.log1p(jnp.exp(-jnp.abs(z)))
    la_ref[...] = log_sig * (1.0 / GLA_GATE_TAU)


def _gla_in(x, gain, w, w_alpha, b_alpha, memq_gain, dqk, dv, dmem):
    t, d = x.shape
    tm = min(TOKEN_TILE, t)
    row = lambda n: pl.BlockSpec((tm, n), lambda i: (i, 0))
    outs = [(dqk, BF16), (dqk, BF16), (dv, BF16), (dv, BF16), (dmem, BF16), (dqk, F32)]
    return pl.pallas_call(
        functools.partial(_gla_in_body, dqk=dqk, dv=dv, dmem=dmem),
        grid=(t // tm,),
        in_specs=[row(d), _const_spec((1, d)), _const_spec(w.shape), _const_spec(w_alpha.shape),
                  _const_spec((1, dqk)), _const_spec((1, MEM_DH))],
        out_specs=[row(n) for n, _ in outs],
        out_shape=[jax.ShapeDtypeStruct((t, n), dt) for n, dt in outs],
        compiler_params=_params("parallel"),
        name="gla_in",
    )(x, gain, w, w_alpha, b_alpha, memq_gain)


def _gla_body(q_ref, k_ref, v_ref, la_ref, r_ref, gn_ref, o_ref, s_ref, *, rows, dk, dv):
    c = GLA_CHUNK

    @pl.when(pl.program_id(1) == 0)
    def _():
        s_ref[...] = jnp.zeros_like(s_ref)

    ri = lax.broadcasted_iota(jnp.int32, (c, c), 0)
    ci = lax.broadcasted_iota(jnp.int32, (c, c), 1)
    causal = ci <= ri
    tril = causal.astype(F32)
    gn = gn_ref[...]

    def chunk(ic, carry):
        r0 = pl.multiple_of(ic * c, c)
        rs = pl.ds(r0, c)
        for h in range(GLA_HEADS):
            ks = slice(h * dk, (h + 1) * dk)
            vs = slice(h * dv, (h + 1) * dv)
            bc = jnp.dot(tril, la_ref[rs, ks], precision=lax.Precision.HIGHEST, preferred_element_type=F32)
            b_last = bc[c - 1:c, :]
            b_mid = bc[c // 2 - 1:c // 2, :]
            q = q_ref[rs, ks].astype(F32) * (dk ** -0.5)
            k = k_ref[rs, ks].astype(F32)
            v = v_ref[rs, vs]
            s_prev = s_ref[h]
            qe = (q * jnp.exp(bc - b_mid)).astype(BF16)
            ke = (k * jnp.exp(b_mid - bc)).astype(BF16)
            sc = jnp.where(causal, _dot_nt(qe, ke), 0.0).astype(BF16)
            o = _dot(sc, v)
            o = o + _dot((q * jnp.exp(bc)).astype(BF16), s_prev.astype(BF16))
            u = _dot_tn((k * jnp.exp(b_last - bc)).astype(BF16), v)
            dcol = jnp.exp(jnp.broadcast_to(b_last, (dk, dk))).T
            decay = jnp.concatenate([dcol] * (dv // dk), axis=1)
            s_ref[h] = decay * s_prev + u
            r = r_ref[rs, vs].astype(F32)
            o_ref[rs, vs] = (_rms(o, gn) * (r * jax.nn.sigmoid(r))).astype(BF16)
        return carry

    lax.fori_loop(0, rows // c, chunk, 0)


def _gla(q, k, v, la, r, out_gain, batch, seq):
    dqk = q.shape[1]
    dvt = v.shape[1]
    dk = dqk // GLA_HEADS
    dv = dvt // GLA_HEADS
    rows = min(GLA_ROWS, seq)
    nb = seq // rows
    row = lambda n: pl.BlockSpec((rows, n), lambda b, i: (b * nb + i, 0))
    return pl.pallas_call(
        functools.partial(_gla_body, rows=rows, dk=dk, dv=dv),
        grid=(batch, nb),
        in_specs=[row(dqk), row(dqk), row(dvt), row(dqk), row(dvt), _const_spec((1, dv))],
        out_specs=row(dvt),
        out_shape=jax.ShapeDtypeStruct((batch * seq, dvt), BF16),
        scratch_shapes=[pltpu.VMEM((GLA_HEADS, dk, dv), F32)],
        compiler_params=_params("parallel", "arbitrary"),
        name="gla_scan",
    )(q, k, v, la, r, out_gain)


def _rope(x, cos, sin):
    return x * cos + pltpu.roll(x, LANES // 2, axis=1) * sin


def _mla_in_body(x_ref, g_ref, w_ref, qn_ref, kvn_ref, wuq_ref, wukv_ref, qg_ref, kg_ref, krg_ref, mqg_ref,
                 cos_ref, sin_ref, q_out, k_out, v_out, qm_out):
    h = _rms(x_ref[...], g_ref[...]).astype(BF16)
    p = _dot(h, w_ref[...])
    o1 = MLA_Q_RANK
    o2 = o1 + MLA_KV_RANK
    o3 = o2 + LANES
    cq = _rms(p[:, :o1], qn_ref[...]).astype(BF16)
    ckv = _rms(p[:, o1:o2], kvn_ref[...]).astype(BF16)
    qm_out[...] = _mem_q(p[:, o3:], mqg_ref[...])
    cos = cos_ref[...]
    sin = sin_ref[...]
    k_rope = _rope(_rms(p[:, o2:o3], krg_ref[...], denom=MLA_ROPE), cos, sin).astype(BF16)
    q = _dot(cq, wuq_ref[...])
    kv = _dot(ckv, wukv_ref[...])
    nk = MLA_HEADS * MLA_NOPE
    v_out[...] = kv[:, nk:].astype(BF16)
    scale = (MLA_NOPE + MLA_ROPE) ** -0.5
    qg = qg_ref[...]
    for hd in range(MLA_HEADS):
        b0 = hd * MLA_QK_PAD
        qn = _rms(q[:, b0:b0 + LANES], qg[:, :LANES]) * scale
        qr = _rope(_rms(q[:, b0 + LANES:b0 + MLA_QK_PAD], qg[:, LANES:], denom=MLA_ROPE), cos, sin) * scale
        q_out[:, b0:b0 + LANES] = qn.astype(BF16)
        q_out[:, b0 + LANES:b0 + MLA_QK_PAD] = qr.astype(BF16)
        kn = _rms(kv[:, hd * MLA_NOPE:(hd + 1) * MLA_NOPE], kg_ref[...])
        k_out[:, b0:b0 + LANES] = kn.astype(BF16)
        k_out[:, b0 + LANES:b0 + MLA_QK_PAD] = k_rope


def _mla_in(x, gain, w, q_norm, kv_norm, w_uq, w_ukv, q_gain, k_gain, kr_gain, memq_gain, cos, sin, seq):
    t, d = x.shape
    tm = min(TOKEN_TILE, t, seq)
    nb = seq // tm
    row = lambda n: pl.BlockSpec((tm, n), lambda i: (i, 0))
    pos = pl.BlockSpec((tm, LANES), lambda i: (i % nb, 0))
    dq = MLA_HEADS * MLA_QK_PAD
    dvt = MLA_HEADS * MLA_DV
    dmem = MEM_HEADS * MEM_DH
    outs = [(dq, BF16), (dq, BF16), (dvt, BF16), (dmem, BF16)]
    consts = [gain, w, q_norm, kv_norm, w_uq, w_ukv, q_gain, k_gain, kr_gain, memq_gain]
    return pl.pallas_call(
        _mla_in_body,
        grid=(t // tm,),
        in_specs=[row(d)] + [_const_spec(a.shape) for a in consts] + [pos, pos],
        out_specs=[row(n) for n, _ in outs],
        out_shape=[jax.ShapeDtypeStruct((t, n), dt) for n, dt in outs],
        compiler_params=_params("parallel"),
        name="mla_in",
    )(x, *consts, cos, sin)


def _flash_body(q_ref, k_ref, v_ref, o_ref, *, tq, tk):
    qi = pl.program_id(2)
    q = q_ref[...]
    n_diag = tq // tk
    n_full = qi * n_diag

    def step(j, carry, masked):
        m, l, acc = carry
        k0 = pl.multiple_of(j * tk, tk)
        s = _dot_nt(q, k_ref[pl.ds(k0, tk), :])
        if masked:
            rpos = qi * tq + lax.broadcasted_iota(jnp.int32, (tq, tk), 0)
            cpos = k0 + lax.broadcasted_iota(jnp.int32, (tq, tk), 1)
            s = jnp.where(cpos <= rpos, s, NEG_BIG)
        m_new = jnp.maximum(m, jnp.max(s, axis=-1, keepdims=True))
        alpha = jnp.exp(m - m_new)
        p = jnp.exp(s - m_new)
        l = alpha * l + jnp.sum(p, axis=-1, keepdims=True)
        acc = alpha * acc + _dot(p.astype(BF16), v_ref[pl.ds(k0, tk), :])
        return m_new, l, acc

    init = (jnp.full((tq, 1), NEG_BIG, F32), jnp.zeros((tq, 1), F32), jnp.zeros((tq, v_ref.shape[-1]), F32))
    carry = lax.fori_loop(0, n_full, lambda j, cr: step(j, cr, False), init)
    for dgn in range(n_diag):
        carry = step(n_full + dgn, carry, True)
    _, l, acc = carry
    o_ref[...] = (acc / l).astype(o_ref.dtype)


def _flash(q, k, v, batch, seq):
    tq = min(FLASH_TQ, seq)
    tk = min(FLASH_TK, tq)
    q3 = q.reshape(batch, seq, -1)
    k3 = k.reshape(batch, seq, -1)
    v3 = v.reshape(batch, seq, -1)
    out = pl.pallas_call(
        functools.partial(_flash_body, tq=tq, tk=tk),
        grid=(batch, MLA_HEADS, seq // tq),
        in_specs=[pl.BlockSpec((None, tq, MLA_QK_PAD), lambda b, h, i: (b, i, h)),
                  pl.BlockSpec((None, seq, MLA_QK_PAD), lambda b, h, i: (b, 0, h)),
                  pl.BlockSpec((None, seq, MLA_DV), lambda b, h, i: (b, 0, h))],
        out_specs=pl.BlockSpec((None, tq, MLA_DV), lambda b, h, i: (b, i, h)),
        out_shape=jax.ShapeDtypeStruct(v3.shape, BF16),
        compiler_params=_params("parallel", "parallel", "arbitrary"),
        name="mla_flash",
    )(q3, k3, v3)
    return out.reshape(batch * seq, -1)


def _memkv_body(mem_ref, g_ref, w_ref, kg_ref, k_out, v_out):
    m = _rms(mem_ref[...], g_ref[...]).astype(BF16)
    kv = _dot(m, w_ref[...])
    dm = MEM_HEADS * MEM_DH
    heads = [_rms(kv[:, h * MEM_DH:(h + 1) * MEM_DH], kg_ref[...]) for h in range(MEM_HEADS)]
    k_out[...] = jnp.concatenate(heads, axis=-1).astype(BF16)
    v_out[...] = kv[:, dm:].astype(BF16)


def _memkv(mem, gain, w_kv, k_gain):
    b, m, d = mem.shape
    dm = MEM_HEADS * MEM_DH
    blk = pl.BlockSpec((None, m, dm), lambda i: (i, 0, 0))
    return pl.pallas_call(
        _memkv_body,
        grid=(b,),
        in_specs=[pl.BlockSpec((None, m, d), lambda i: (i, 0, 0)), _const_spec((1, d)),
                  _const_spec(w_kv.shape), _const_spec((1, MEM_DH))],
        out_specs=[blk, blk],
        out_shape=[jax.ShapeDtypeStruct((b, m, dm), BF16)] * 2,
        compiler_params=_params("parallel"),
        name="mem_kv",
    )(mem, gain, w_kv, k_gain)


def _out_body(x_ref, o_ref, qm_ref, km_ref, vm_ref, wo_ref, out_ref):
    qm = qm_ref[...]
    km = km_ref[...]
    vm = vm_ref[...]
    heads = []
    for h in range(MEM_HEADS):
        hs = slice(h * MEM_DH, (h + 1) * MEM_DH)
        s = _dot_nt(qm[:, hs], km[:, hs])
        p = jnp.exp(s - jnp.max(s, axis=-1, keepdims=True))
        l = jnp.sum(p, axis=-1, keepdims=True)
        heads.append((_dot(p.astype(BF16), vm[:, hs]) / l).astype(BF16))
    om = jnp.concatenate(heads, axis=-1)
    dmix = o_ref.shape[-1]
    y = _dot(o_ref[...], wo_ref[:dmix, :]) + _dot(om, wo_ref[dmix:, :])
    out_ref[...] = x_ref[...] + y


def _out_proj(x, o, q_mem, k_mem, v_mem, w_out, seq):
    t, d = x.shape
    tm = min(TOKEN_TILE, t, seq)
    nb = seq // tm
    row = lambda n: pl.BlockSpec((tm, n), lambda i: (i, 0))
    mem_blk = pl.BlockSpec((None,) + k_mem.shape[1:], lambda i: (i // nb, 0, 0))
    return pl.pallas_call(
        _out_body,
        grid=(t // tm,),
        in_specs=[row(d), row(o.shape[1]), row(q_mem.shape[1]), mem_blk, mem_blk, _const_spec(w_out.shape)],
        out_specs=row(d),
        out_shape=jax.ShapeDtypeStruct((t, d), F32),
        compiler_params=_params("parallel"),
        name="out_proj",
    )(x, o, q_mem, k_mem, v_mem, w_out)


def _pad_rope(w):
    half = MLA_ROPE // 2
    z = jnp.zeros(w.shape[:-1] + (LANES // 2 - half,), w.dtype)
    return jnp.concatenate([w[..., :half], z, w[..., half:], z], axis=-1)


def _rope_tables(seq):
    inv = 1.0 / (ROPE_BASE ** (jnp.arange(0, MLA_ROPE, 2, dtype=F32) / MLA_ROPE))
    ang = jnp.arange(seq, dtype=F32)[:, None] * inv[None, :]
    cos = _pad_rope(jnp.concatenate([jnp.cos(ang)] * 2, axis=-1))
    sin = _pad_rope(jnp.concatenate([-jnp.sin(ang), jnp.sin(ang)], axis=-1))
    return cos, sin


def kernel(x, mem, ffn1_norm, ffn1_w_gate, ffn1_w_up, ffn1_w_down, ffn2_norm, ffn2_w_gate, ffn2_w_up, ffn2_w_down, mix_norm, w_out, mem_norm, mem_w_kv, memq_norm, memk_norm, gla_w_in, gla_w_alpha, gla_b_alpha, gla_out_norm, mla_w_in, mla_q_norm, mla_kv_norm, mla_w_uq, mla_w_ukv, mla_qn_norm, mla_qr_norm, mla_kn_norm, mla_kr_norm):
    batch, seq, d = x.shape
    depth = ffn1_norm.shape[0]
    t = batch * seq
    row = lambda a: a.reshape(1, -1).astype(F32)
    xs = x.reshape(t, d)
    cos, sin = _rope_tables(seq)

    gla_dqk = gla_w_alpha.shape[-1]
    gla_dv = d
    dmem = MEM_HEADS * MEM_DH

    for i in range(depth):
        j = i // 2
        xs = _ffn(xs, row(ffn1_norm[i]), ffn1_w_gate[i].astype(BF16), ffn1_w_up[i].astype(BF16),
                  ffn1_w_down[i].astype(BF16))
        k_mem, v_mem = _memkv(mem, row(mem_norm[i]), mem_w_kv[i].astype(BF16), row(memk_norm[i]))
        if i % 2 == 0:
            wq, wk, wv, wa, wr, wm = jnp.split(gla_w_in[j], [gla_dqk, 2 * gla_dqk, 2 * gla_dqk + gla_dv,
                                                              2 * gla_dqk + gla_dv + GLA_GATE_RANK,
                                                              2 * gla_dqk + 2 * gla_dv + GLA_GATE_RANK], axis=-1)
            wa = jnp.pad(wa, ((0, 0), (0, LANES - GLA_GATE_RANK)))
            w_in = jnp.concatenate([wq, wk, wv, wr, wm, wa], axis=-1).astype(BF16)
            w_alpha = jnp.pad(gla_w_alpha[j], ((0, LANES - GLA_GATE_RANK), (0, 0))).astype(BF16)
            q, k, v, r, q_mem, la = _gla_in(xs, row(mix_norm[i]), w_in, w_alpha, row(gla_b_alpha[j]),
                                            row(memq_norm[i]), gla_dqk, gla_dv, dmem)
            o = _gla(q, k, v, la, r, row(gla_out_norm[j]), batch, seq)
        else:
            wcq, wckv, wkr, wm = jnp.split(mla_w_in[j], [MLA_Q_RANK, MLA_Q_RANK + MLA_KV_RANK,
                                                          MLA_Q_RANK + MLA_KV_RANK + MLA_ROPE], axis=-1)
            w_in = jnp.concatenate([wcq, wckv, _pad_rope(wkr), wm], axis=-1).astype(BF16)
            uq = mla_w_uq[j].reshape(MLA_Q_RANK, MLA_HEADS, MLA_NOPE + MLA_ROPE)
            w_uq = jnp.concatenate([uq[..., :MLA_NOPE], _pad_rope(uq[..., MLA_NOPE:])], axis=-1)
            w_uq = w_uq.reshape(MLA_Q_RANK, MLA_HEADS * MLA_QK_PAD).astype(BF16)
            ukv = mla_w_ukv[j].reshape(MLA_KV_RANK, MLA_HEADS, MLA_NOPE + MLA_DV)
            w_ukv = jnp.concatenate([ukv[..., :MLA_NOPE].reshape(MLA_KV_RANK, -1),
                                     ukv[..., MLA_NOPE:].reshape(MLA_KV_RANK, -1)], axis=-1).astype(BF16)
            q_gain = jnp.concatenate([mla_qn_norm[j], _pad_rope(mla_qr_norm[j])])
            q, k, v, q_mem = _mla_in(xs, row(mix_norm[i]), w_in, row(mla_q_norm[j]), row(mla_kv_norm[j]),
                                     w_uq, w_ukv, row(q_gain), row(mla_kn_norm[j]),
                                     row(_pad_rope(mla_kr_norm[j])), row(memq_norm[i]), cos, sin, seq)
            o = _flash(q, k, v, batch, seq)
        xs = _out_proj(xs, o, q_mem, k_mem, v_mem, w_out[i].astype(BF16), seq)
        xs = _ffn(xs, row(ffn2_norm[i]), ffn2_w_gate[i].astype(BF16), ffn2_w_up[i].astype(BF16),
                  ffn2_w_down[i].astype(BF16))
    return xs.reshape(batch, seq, d)
```

```python
import functools

import jax
import jax.numpy as jnp
from jax import lax
from jax.experimental import pallas as pl
from jax.experimental.pallas import tpu as pltpu

F32 = jnp.float32
BF16 = jnp.bfloat16

EPS = 1e-6
GLA_HEADS = 4
GLA_GATE_RANK = 16
GLA_GATE_TAU = 16.0
GLA_CHUNK = 64
MLA_HEADS = 8
MLA_Q_RANK = 384
MLA_KV_RANK = 256
MLA_NOPE = 128
MLA_ROPE = 64
MLA_DV = 128
ROPE_BASE = 10000.0
MEM_HEADS = 4
MEM_DH = 128

LANES = 128
MLA_QK_PAD = 2 * LANES
VMEM_LIMIT_BYTES = 56 * 1024 * 1024

TOKEN_TILE = 512
FLASH_TQ = 2048
FLASH_TK = 512
FLASH_CHAIN_ROWS = 256
GLA_ROWS = 512

NEG_BIG = -1e30
LOG2_E = 1.4426950408889634


def _params(*sem):
    return pltpu.CompilerParams(dimension_semantics=sem, vmem_limit_bytes=VMEM_LIMIT_BYTES)


def _layer_spec(stacked, layer):
    nd = stacked.ndim - 1
    return pl.BlockSpec((None,) + stacked.shape[1:], lambda *_: (layer,) + (0,) * nd,
                        pipeline_mode=pl.Buffered(1))


def _layered(pairs):
    return [_layer_spec(a, l) for a, l in pairs], [a for a, _ in pairs]


def _rms(x, gain, denom=None):
    d = x.shape[-1] if denom is None else denom
    ms = jnp.sum(x * x, axis=-1, keepdims=True) * (1.0 / d)
    return x * lax.rsqrt(ms + EPS) * gain


def _dot(a, b):
    return jnp.dot(a, b, preferred_element_type=F32)


def _dot_nt(a, b):
    return lax.dot_general(a, b, (((1,), (1,)), ((), ())), preferred_element_type=F32)


def _dot_tn(a, b):
    return lax.dot_general(a, b, (((0,), (0,)), ((), ())), preferred_element_type=F32)


def _ffn_body(x_ref, g_ref, wg_ref, wu_ref, wd_ref, o_ref):
    x = x_ref[...]
    h = _rms(x, g_ref[...]).astype(BF16)
    gate = _dot(h, wg_ref[...])
    up = _dot(h, wu_ref[...])
    a = (gate * jax.nn.sigmoid(gate) * up).astype(BF16)
    o_ref[...] = x + 0.5 * _dot(a, wd_ref[...])


def _ffn(x, params):
    t, d = x.shape
    tm = min(TOKEN_TILE, t)
    row = pl.BlockSpec((tm, d), lambda i: (i, 0))
    specs, ops = _layered(params)
    return pl.pallas_call(
        _ffn_body,
        grid=(t // tm,),
        in_specs=[row] + specs,
        out_specs=row,
        out_shape=jax.ShapeDtypeStruct((t, d), F32),
        compiler_params=_params("parallel"),
        name="ffn",
    )(x, *ops)


def _mem_q(p, gain):
    heads = [_rms(p[:, h * MEM_DH:(h + 1) * MEM_DH], gain) * (MEM_DH ** -0.5) for h in range(MEM_HEADS)]
    return jnp.concatenate(heads, axis=-1).astype(BF16)


def _gla_in_body(x_ref, g_ref, w_ref, wa_ref, ba_ref, mqg_ref,
                 q_ref, k_ref, v_ref, r_ref, qm_ref, la_ref, *, dqk, dv, dmem):
    h = _rms(x_ref[...], g_ref[...]).astype(BF16)
    p = _dot(h, w_ref[...])
    o = 0
    q_ref[...] = p[:, o:o + dqk].astype(BF16); o += dqk
    k_ref[...] = p[:, o:o + dqk].astype(BF16); o += dqk
    v_ref[...] = p[:, o:o + dv].astype(BF16); o += dv
    r_ref[...] = p[:, o:o + dv].astype(BF16); o += dv
    qm_ref[...] = _mem_q(p[:, o:o + dmem], mqg_ref[...]); o += dmem
    a_low = p[:, o:o + LANES].astype(BF16)
    z = _dot(a_low, wa_ref[...]) + ba_ref[...]
    log_sig = jnp.minimum(z, 0.0) - jnp.log1p(jnp.exp(-jnp.abs(z)))
    la_ref[...] = log_sig * (1.0 / GLA_GATE_TAU)


def _gla_in(x, params, dqk, dv, dmem):
    t, d = x.shape
    tm = min(TOKEN_TILE, t)
    row = lambda n: pl.BlockSpec((tm, n), lambda i: (i, 0))
    outs = [(dqk, BF16), (dqk, BF16), (dv, BF16), (dv, BF16), (dmem, BF16), (dqk, F32)]
    specs, ops = _layered(params)
    return pl.pallas_call(
        functools.partial(_gla_in_body, dqk=dqk, dv=dv, dmem=dmem),
        grid=(t // tm,),
        in_specs=[row(d)] + specs,
        out_specs=[row(n) for n, _ in outs],
        out_shape=[jax.ShapeDtypeStruct((t, n), dt) for n, dt in outs],
        compiler_params=_params("parallel"),
        name="gla_in",
    )(x, *ops)


def _gla_body(q_ref, k_ref, v_ref, la_ref, r_ref, gn_ref, o_ref, s_ref, *, rows, dk, dv):
    c = GLA_CHUNK

    @pl.when(pl.program_id(1) == 0)
    def _():
        s_ref[...] = jnp.zeros_like(s_ref)

    ri = lax.broadcasted_iota(jnp.int32, (c, c), 0)
    ci = lax.broadcasted_iota(jnp.int32, (c, c), 1)
    causal = ci <= ri
    tril = causal.astype(F32)
    gn = gn_ref[...]

    for ic in range(rows // c):
        rs = slice(ic * c, (ic + 1) * c)
        bc_all = jnp.dot(tril, la_ref[rs, :], precision=lax.Precision.HIGHEST, preferred_element_type=F32)
        for h in range(GLA_HEADS):
            ks = slice(h * dk, (h + 1) * dk)
            vs = slice(h * dv, (h + 1) * dv)
            bc = bc_all[:, ks]
            b_last = bc[c - 1:c, :]
            b_mid = bc[c // 2 - 1:c // 2, :]
            q = q_ref[rs, ks].astype(F32) * (dk ** -0.5)
            k = k_ref[rs, ks].astype(F32)
            v = v_ref[rs, vs]
            s_prev = s_ref[h]
            qe = (q * jnp.exp(bc - b_mid)).astype(BF16)
            ke = (k * jnp.exp(b_mid - bc)).astype(BF16)
            sc = jnp.where(causal, _dot_nt(qe, ke), 0.0).astype(BF16)
            o = _dot(sc, v)
            o = o + _dot((q * jnp.exp(bc)).astype(BF16), s_prev.astype(BF16))
            u = _dot_tn((k * jnp.exp(b_last - bc)).astype(BF16), v)
            dcol = jnp.exp(jnp.broadcast_to(b_last, (dk, dk))).T
            decay = jnp.concatenate([dcol] * (dv // dk), axis=1)
            s_ref[h] = decay * s_prev + u
            r = r_ref[rs, vs].astype(F32)
            o_ref[rs, vs] = (_rms(o, gn) * (r * jax.nn.sigmoid(r))).astype(BF16)


def _gla(q, k, v, la, r, out_gain, batch, seq):
    dqk = q.shape[1]
    dvt = v.shape[1]
    dk = dqk // GLA_HEADS
    dv = dvt // GLA_HEADS
    rows = min(GLA_ROWS, seq)
    nb = seq // rows
    row = lambda n: pl.BlockSpec((rows, n), lambda b, i: (b * nb + i, 0))
    return pl.pallas_call(
        functools.partial(_gla_body, rows=rows, dk=dk, dv=dv),
        grid=(batch, nb),
        in_specs=[row(dqk), row(dqk), row(dvt), row(dqk), row(dvt), _layer_spec(*out_gain)],
        out_specs=row(dvt),
        out_shape=jax.ShapeDtypeStruct((batch * seq, dvt), BF16),
        scratch_shapes=[pltpu.VMEM((GLA_HEADS, dk, dv), F32)],
        compiler_params=_params("parallel", "arbitrary"),
        name="gla_scan",
    )(q, k, v, la, r, out_gain[0])


def _rope(x, cos, sin):
    return x * cos + pltpu.roll(x, LANES // 2, axis=1) * sin


def _mla_in_body(x_ref, g_ref, w_ref, qn_ref, kvn_ref, wuq_ref, wukv_ref, qg_ref, kg_ref, krg_ref, mqg_ref,
                 cos_ref, sin_ref, q_out, k_out, v_out, qm_out):
    h = _rms(x_ref[...], g_ref[...]).astype(BF16)
    p = _dot(h, w_ref[...])
    o1 = MLA_Q_RANK
    o2 = o1 + MLA_KV_RANK
    o3 = o2 + LANES
    cq = _rms(p[:, :o1], qn_ref[...]).astype(BF16)
    ckv = _rms(p[:, o1:o2], kvn_ref[...]).astype(BF16)
    qm_out[...] = _mem_q(p[:, o3:], mqg_ref[...])
    cos = cos_ref[...]
    sin = sin_ref[...]
    k_rope = _rope(_rms(p[:, o2:o3], krg_ref[...], denom=MLA_ROPE), cos, sin).astype(BF16)
    q = _dot(cq, wuq_ref[...])
    kv = _dot(ckv, wukv_ref[...])
    nk = MLA_HEADS * MLA_NOPE
    v_out[...] = kv[:, nk:].astype(BF16)
    scale = (MLA_NOPE + MLA_ROPE) ** -0.5 * LOG2_E
    qg = qg_ref[...]
    for hd in range(MLA_HEADS):
        b0 = hd * MLA_QK_PAD
        qn = _rms(q[:, b0:b0 + LANES], qg[:, :LANES]) * scale
        qr = _rope(_rms(q[:, b0 + LANES:b0 + MLA_QK_PAD], qg[:, LANES:], denom=MLA_ROPE), cos, sin) * scale
        q_out[:, b0:b0 + LANES] = qn.astype(BF16)
        q_out[:, b0 + LANES:b0 + MLA_QK_PAD] = qr.astype(BF16)
        kn = _rms(kv[:, hd * MLA_NOPE:(hd + 1) * MLA_NOPE], kg_ref[...])
        k_out[:, b0:b0 + LANES] = kn.astype(BF16)
        k_out[:, b0 + LANES:b0 + MLA_QK_PAD] = k_rope


def _mla_in(x, params, cos, sin, seq):
    t, d = x.shape
    tm = min(TOKEN_TILE, t, seq)
    nb = seq // tm
    row = lambda n: pl.BlockSpec((tm, n), lambda i: (i, 0))
    pos = pl.BlockSpec((tm, LANES), lambda i: (i % nb, 0))
    dq = MLA_HEADS * MLA_QK_PAD
    dvt = MLA_HEADS * MLA_DV
    dmem = MEM_HEADS * MEM_DH
    outs = [(dq, BF16), (dq, BF16), (dvt, BF16), (dmem, BF16)]
    specs, ops = _layered(params)
    return pl.pallas_call(
        _mla_in_body,
        grid=(t // tm,),
        in_specs=[row(d)] + specs + [pos, pos],
        out_specs=[row(n) for n, _ in outs],
        out_shape=[jax.ShapeDtypeStruct((t, n), dt) for n, dt in outs],
        compiler_params=_params("parallel"),
        name="mla_in",
    )(x, *ops, cos, sin)


def _flash_body(q_ref, k_ref, v_ref, o_ref, *, tq, tk, rc):
    qi = pl.program_id(2)
    n_diag = tq // tk
    n_chain = tq // rc
    dv = v_ref.shape[-1]
    ones = jnp.ones((tk, LANES), BF16)

    def kv_step(j, state, diag):
        k0 = pl.multiple_of(j * tk, tk)
        kb = k_ref[pl.ds(k0, tk), :]
        vb = jnp.concatenate([v_ref[pl.ds(k0, tk), :], ones], axis=1)
        new = []
        for c in range(n_chain):
            m, acc = state[c]
            lo_row, hi_row = c * rc, (c + 1) * rc - 1
            masked = False
            if diag is not None:
                lo_col, hi_col = diag * tk, (diag + 1) * tk - 1
                if lo_col > hi_row:
                    new.append((m, acc))
                    continue
                masked = hi_col > lo_row
                assert not masked or lo_col <= lo_row
            s = _dot_nt(q_ref[lo_row:hi_row + 1, :], kb)
            if masked:
                rpos = lo_row + lax.broadcasted_iota(jnp.int32, (rc, tk), 0)
                cpos = lo_col + lax.broadcasted_iota(jnp.int32, (rc, tk), 1)
                s = jnp.where(cpos <= rpos, s, NEG_BIG)
            m_new = jnp.maximum(m, jnp.max(s, axis=-1, keepdims=True))
            alpha = jnp.exp2(m - m_new)
            p = jnp.exp2(s - m_new).astype(BF16)
            new.append((m_new, alpha * acc + _dot(p, vb)))
        return new

    def full_blocks(t, state):
        for d in range(n_diag):
            state = kv_step(t * n_diag + d, state, None)
        return tuple(state)

    init = tuple((jnp.full((rc, 1), NEG_BIG, F32), jnp.zeros((rc, dv + LANES), F32)) for _ in range(n_chain))
    state = lax.fori_loop(0, qi, full_blocks, init)
    for d in range(n_diag):
        state = kv_step(qi * n_diag + d, state, d)
    for c in range(n_chain):
        acc = state[c][1]
        o_ref[c * rc:(c + 1) * rc, :] = (acc[:, :dv] / acc[:, dv:]).astype(o_ref.dtype)


def _flash(q, k, v, batch, seq):
    tq = min(FLASH_TQ, seq)
    tk = min(FLASH_TK, tq)
    rc = min(FLASH_CHAIN_ROWS, tq)
    q3 = q.reshape(batch, seq, -1)
    k3 = k.reshape(batch, seq, -1)
    v3 = v.reshape(batch, seq, -1)
    out = pl.pallas_call(
        functools.partial(_flash_body, tq=tq, tk=tk, rc=rc),
        grid=(batch, MLA_HEADS, seq // tq),
        in_specs=[pl.BlockSpec((None, tq, MLA_QK_PAD), lambda b, h, i: (b, i, h)),
                  pl.BlockSpec((None, seq, MLA_QK_PAD), lambda b, h, i: (b, 0, h)),
                  pl.BlockSpec((None, seq, MLA_DV), lambda b, h, i: (b, 0, h))],
        out_specs=pl.BlockSpec((None, tq, MLA_DV), lambda b, h, i: (b, i, h)),
        out_shape=jax.ShapeDtypeStruct(v3.shape, BF16),
        compiler_params=_params("parallel", "parallel", "arbitrary"),
        name="mla_flash",
    )(q3, k3, v3)
    return out.reshape(batch * seq, -1)


def _memkv_body(mem_ref, g_ref, w_ref, kg_ref, k_out, v_out):
    m = _rms(mem_ref[...], g_ref[...]).astype(BF16)
    kv = _dot(m, w_ref[...])
    dm = MEM_HEADS * MEM_DH
    heads = [_rms(kv[:, h * MEM_DH:(h + 1) * MEM_DH], kg_ref[...]) for h in range(MEM_HEADS)]
    k_out[...] = jnp.concatenate(heads, axis=-1).astype(BF16)
    v_out[...] = kv[:, dm:].astype(BF16)


def _memkv(mem, params):
    b, m, d = mem.shape
    specs, ops = _layered(params)
    dm = MEM_HEADS * MEM_DH
    blk = pl.BlockSpec((None, m, dm), lambda i: (i, 0, 0))
    return pl.pallas_call(
        _memkv_body,
        grid=(b,),
        in_specs=[pl.BlockSpec((None, m, d), lambda i: (i, 0, 0))] + specs,
        out_specs=[blk, blk],
        out_shape=[jax.ShapeDtypeStruct((b, m, dm), BF16)] * 2,
        compiler_params=_params("parallel"),
        name="mem_kv",
    )(mem, *ops)


def _out_body(x_ref, o_ref, qm_ref, km_ref, vm_ref, wo_ref, out_ref):
    qm = qm_ref[...]
    km = km_ref[...]
    vm = vm_ref[...]
    heads = []
    for h in range(MEM_HEADS):
        hs = slice(h * MEM_DH, (h + 1) * MEM_DH)
        s = _dot_nt(qm[:, hs], km[:, hs])
        p = jnp.exp(s - jnp.max(s, axis=-1, keepdims=True))
        l = jnp.sum(p, axis=-1, keepdims=True)
        heads.append((_dot(p.astype(BF16), vm[:, hs]) / l).astype(BF16))
    om = jnp.concatenate(heads, axis=-1)
    dmix = o_ref.shape[-1]
    y = _dot(o_ref[...], wo_ref[:dmix, :]) + _dot(om, wo_ref[dmix:, :])
    out_ref[...] = x_ref[...] + y


def _out_proj(x, o, q_mem, k_mem, v_mem, w_out, seq):
    t, d = x.shape
    tm = min(TOKEN_TILE, t, seq)
    nb = seq // tm
    row = lambda n: pl.BlockSpec((tm, n), lambda i: (i, 0))
    mem_blk = pl.BlockSpec((None,) + k_mem.shape[1:], lambda i: (i // nb, 0, 0))
    return pl.pallas_call(
        _out_body,
        grid=(t // tm,),
        in_specs=[row(d), row(o.shape[1]), row(q_mem.shape[1]), mem_blk, mem_blk, _layer_spec(*w_out)],
        out_specs=row(d),
        out_shape=jax.ShapeDtypeStruct((t, d), F32),
        compiler_params=_params("parallel"),
        name="out_proj",
    )(x, o, q_mem, k_mem, v_mem, w_out[0])


def _pad_rope(w):
    half = MLA_ROPE // 2
    z = jnp.zeros(w.shape[:-1] + (LANES // 2 - half,), w.dtype)
    return jnp.concatenate([w[..., :half], z, w[..., half:], z], axis=-1)


def _rope_tables(seq):
    inv = 1.0 / (ROPE_BASE ** (jnp.arange(0, MLA_ROPE, 2, dtype=F32) / MLA_ROPE))
    ang = jnp.arange(seq, dtype=F32)[:, None] * inv[None, :]
    cos = _pad_rope(jnp.concatenate([jnp.cos(ang)] * 2, axis=-1))
    sin = _pad_rope(jnp.concatenate([-jnp.sin(ang), jnp.sin(ang)], axis=-1))
    return cos, sin


def kernel(x, mem, ffn1_norm, ffn1_w_gate, ffn1_w_up, ffn1_w_down, ffn2_norm, ffn2_w_gate, ffn2_w_up, ffn2_w_down, mix_norm, w_out, mem_norm, mem_w_kv, memq_norm, memk_norm, gla_w_in, gla_w_alpha, gla_b_alpha, gla_out_norm, mla_w_in, mla_q_norm, mla_kv_norm, mla_w_uq, mla_w_ukv, mla_qn_norm, mla_qr_norm, mla_kn_norm, mla_kr_norm):
    batch, seq, d = x.shape
    depth = ffn1_norm.shape[0]
    t = batch * seq
    xs = x.reshape(t, d)
    cos, sin = _rope_tables(seq)
    gains = lambda a: a.astype(F32)[:, None, :]
    bf = lambda a: a.astype(BF16)

    ffn1 = [gains(ffn1_norm), bf(ffn1_w_gate), bf(ffn1_w_up), bf(ffn1_w_down)]
    ffn2 = [gains(ffn2_norm), bf(ffn2_w_gate), bf(ffn2_w_up), bf(ffn2_w_down)]
    mix_g, memq_g = gains(mix_norm), gains(memq_norm)
    memkv = [gains(mem_norm), bf(mem_w_kv), gains(memk_norm)]
    w_out_b = bf(w_out)

    gla_dqk = gla_w_alpha.shape[-1]
    gla_dv = d
    dmem = MEM_HEADS * MEM_DH
    a_lo = 2 * gla_dqk + gla_dv
    a_hi = a_lo + GLA_GATE_RANK
    pad_rank = LANES - GLA_GATE_RANK
    gla_w = bf(jnp.concatenate([gla_w_in[..., :a_lo], gla_w_in[..., a_hi:],
                                jnp.pad(gla_w_in[..., a_lo:a_hi], ((0, 0), (0, 0), (0, pad_rank)))], axis=-1))
    gla_wa = bf(jnp.pad(gla_w_alpha, ((0, 0), (0, pad_rank), (0, 0))))
    gla_ba, gla_og = gains(gla_b_alpha), gains(gla_out_norm)

    r_lo = MLA_Q_RANK + MLA_KV_RANK
    r_hi = r_lo + MLA_ROPE
    n_mla = mla_w_in.shape[0]
    mla_w = bf(jnp.concatenate([mla_w_in[..., :r_lo], _pad_rope(mla_w_in[..., r_lo:r_hi]),
                                mla_w_in[..., r_hi:]], axis=-1))
    uq = mla_w_uq.reshape(n_mla, MLA_Q_RANK, MLA_HEADS, MLA_NOPE + MLA_ROPE)
    w_uq = bf(jnp.concatenate([uq[..., :MLA_NOPE], _pad_rope(uq[..., MLA_NOPE:])], axis=-1)
              .reshape(n_mla, MLA_Q_RANK, MLA_HEADS * MLA_QK_PAD))
    ukv = mla_w_ukv.reshape(n_mla, MLA_KV_RANK, MLA_HEADS, MLA_NOPE + MLA_DV)
    w_ukv = bf(jnp.concatenate([ukv[..., :MLA_NOPE].reshape(n_mla, MLA_KV_RANK, -1),
                                ukv[..., MLA_NOPE:].reshape(n_mla, MLA_KV_RANK, -1)], axis=-1))
    mla_qg = gains(jnp.concatenate([mla_qn_norm, _pad_rope(mla_qr_norm)], axis=-1))
    mla_qn, mla_kvn = gains(mla_q_norm), gains(mla_kv_norm)
    mla_kg, mla_krg = gains(mla_kn_norm), gains(_pad_rope(mla_kr_norm))

    for i in range(depth):
        j = i // 2
        xs = _ffn(xs, [(a, i) for a in ffn1])
        k_mem, v_mem = _memkv(mem, [(a, i) for a in memkv])
        if i % 2 == 0:
            q, k, v, r, q_mem, la = _gla_in(xs, [(mix_g, i), (gla_w, j), (gla_wa, j), (gla_ba, j), (memq_g, i)],
                                            gla_dqk, gla_dv, dmem)
            o = _gla(q, k, v, la, r, (gla_og, j), batch, seq)
        else:
            q, k, v, q_mem = _mla_in(xs, [(mix_g, i), (mla_w, j), (mla_qn, j), (mla_kvn, j), (w_uq, j), (w_ukv, j),
                                          (mla_qg, j), (mla_kg, j), (mla_krg, j), (memq_g, i)], cos, sin, seq)
            o = _flash(q, k, v, batch, seq)
        xs = _out_proj(xs, o, q_mem, k_mem, v_mem, (w_out_b, i), seq)
        xs = _ffn(xs, [(a, i) for a in ffn2])
    return xs.reshape(batch, seq, d)
```

```python
import functools

import jax
import jax.numpy as jnp
from jax import lax
from jax.experimental import pallas as pl
from jax.experimental.pallas import tpu as pltpu

F32 = jnp.float32
BF16 = jnp.bfloat16

EPS = 1e-6
GLA_HEADS = 4
GLA_GATE_RANK = 16
GLA_GATE_TAU = 16.0
GLA_CHUNK = 64
MLA_HEADS = 8
MLA_Q_RANK = 384
MLA_KV_RANK = 256
MLA_NOPE = 128
MLA_ROPE = 64
MLA_DV = 128
ROPE_BASE = 10000.0
MEM_HEADS = 4
MEM_DH = 128

LANES = 128
MLA_QK_PAD = 2 * LANES
VMEM_LIMIT_BYTES = 56 * 1024 * 1024

TOKEN_TILE = 512
FLASH_TQ = 2048
FLASH_TK = 512
FLASH_CHAIN_ROWS = 256
FLASH_LOOKAHEAD = 8
ONES_ROWS = 16
GLA_ROWS = 512
GLA_LOOKAHEAD = 4

NEG_BIG = -1e30
LOG2_E = 1.4426950408889634


def _params(*sem):
    return pltpu.CompilerParams(dimension_semantics=sem, vmem_limit_bytes=VMEM_LIMIT_BYTES)


def _layer_spec(stacked, layer):
    nd = stacked.ndim - 1
    return pl.BlockSpec((None,) + stacked.shape[1:], lambda *_: (layer,) + (0,) * nd,
                        pipeline_mode=pl.Buffered(1))


def _layered(pairs):
    return [_layer_spec(a, l) for a, l in pairs], [a for a, _ in pairs]


def _rms(x, gain, denom=None):
    d = x.shape[-1] if denom is None else denom
    ms = jnp.sum(x * x, axis=-1, keepdims=True) * (1.0 / d)
    return x * lax.rsqrt(ms + EPS) * gain


def _dot(a, b):
    return jnp.dot(a, b, preferred_element_type=F32)


def _dot_nt(a, b):
    return lax.dot_general(a, b, (((1,), (1,)), ((), ())), preferred_element_type=F32)


def _dot_tn(a, b):
    return lax.dot_general(a, b, (((0,), (0,)), ((), ())), preferred_element_type=F32)


def _ffn_body(x_ref, g_ref, wg_ref, wu_ref, wd_ref, o_ref):
    x = x_ref[...]
    h = _rms(x, g_ref[...]).astype(BF16)
    gate = _dot(h, wg_ref[...])
    up = _dot(h, wu_ref[...])
    a = (gate * jax.nn.sigmoid(gate) * up).astype(BF16)
    o_ref[...] = x + 0.5 * _dot(a, wd_ref[...])


def _ffn(x, params):
    t, d = x.shape
    tm = min(TOKEN_TILE, t)
    row = pl.BlockSpec((tm, d), lambda i: (i, 0))
    specs, ops = _layered(params)
    return pl.pallas_call(
        _ffn_body,
        grid=(t // tm,),
        in_specs=[row] + specs,
        out_specs=row,
        out_shape=jax.ShapeDtypeStruct((t, d), F32),
        compiler_params=_params("parallel"),
        name="ffn",
    )(x, *ops)


def _mem_q(p, gain):
    heads = [_rms(p[:, h * MEM_DH:(h + 1) * MEM_DH], gain) * (MEM_DH ** -0.5) for h in range(MEM_HEADS)]
    return jnp.concatenate(heads, axis=-1).astype(BF16)


def _gla_in_body(x_ref, g_ref, w_ref, wa_ref, ba_ref, mqg_ref,
                 q_ref, k_ref, v_ref, r_ref, qm_ref, la_ref, *, dqk, dv, dmem):
    h = _rms(x_ref[...], g_ref[...]).astype(BF16)
    p = _dot(h, w_ref[...])
    o = 0
    q_ref[...] = p[:, o:o + dqk].astype(BF16); o += dqk
    k_ref[...] = p[:, o:o + dqk].astype(BF16); o += dqk
    v_ref[...] = p[:, o:o + dv].astype(BF16); o += dv
    r_ref[...] = p[:, o:o + dv].astype(BF16); o += dv
    qm_ref[...] = _mem_q(p[:, o:o + dmem], mqg_ref[...]); o += dmem
    a_low = p[:, o:o + LANES].astype(BF16)
    z = _dot(a_low, wa_ref[...]) + ba_ref[...]
    log_sig = jnp.minimum(z, 0.0) - jnp.log1p(jnp.exp(-jnp.abs(z)))
    la_ref[...] = log_sig * (1.0 / GLA_GATE_TAU)


def _gla_in(x, params, dqk, dv, dmem):
    t, d = x.shape
    tm = min(TOKEN_TILE, t)
    row = lambda n: pl.BlockSpec((tm, n), lambda i: (i, 0))
    outs = [(dqk, BF16), (dqk, BF16), (dv, BF16), (dv, BF16), (dmem, BF16), (dqk, F32)]
    specs, ops = _layered(params)
    return pl.pallas_call(
        functools.partial(_gla_in_body, dqk=dqk, dv=dv, dmem=dmem),
        grid=(t // tm,),
        in_specs=[row(d)] + specs,
        out_specs=[row(n) for n, _ in outs],
        out_shape=[jax.ShapeDtypeStruct((t, n), dt) for n, dt in outs],
        compiler_params=_params("parallel"),
        name="gla_in",
    )(x, *ops)


def _gla_body(q_ref, k_ref, v_ref, la_ref, r_ref, gn_ref, o_ref, s_ref, *, rows, dk, dv):
    c = GLA_CHUNK

    @pl.when(pl.program_id(1) == 0)
    def _():
        s_ref[...] = jnp.zeros_like(s_ref)

    ri = lax.broadcasted_iota(jnp.int32, (c, c), 0)
    ci = lax.broadcasted_iota(jnp.int32, (c, c), 1)
    causal = ci <= ri
    tril = causal.astype(F32)
    gn = gn_ref[...]

    items = [(ic, h) for ic in range(rows // c) for h in range(GLA_HEADS)]
    prefix = {}

    def cum_decay(ic):
        if ic not in prefix:
            prefix[ic] = jnp.dot(tril, la_ref[ic * c:(ic + 1) * c, :], precision=lax.Precision.HIGHEST,
                                 preferred_element_type=F32)
        return prefix[ic]

    def independent_part(item):
        ic, h = item
        rs = slice(ic * c, (ic + 1) * c)
        ks = slice(h * dk, (h + 1) * dk)
        bc = cum_decay(ic)[:, ks]
        b_last = bc[c - 1:c, :]
        b_mid = bc[c // 2 - 1:c // 2, :]
        q = q_ref[rs, ks].astype(F32) * (dk ** -0.5)
        k = k_ref[rs, ks].astype(F32)
        v = v_ref[rs, h * dv:(h + 1) * dv]
        qe = (q * jnp.exp(bc - b_mid)).astype(BF16)
        ke = (k * jnp.exp(b_mid - bc)).astype(BF16)
        sc = _dot_nt(qe, ke)
        u = _dot_tn((k * jnp.exp(b_last - bc)).astype(BF16), v)
        qi = (q * jnp.exp(bc)).astype(BF16)
        dcol = jnp.exp(jnp.broadcast_to(b_last, (dk, dk))).T
        return sc, u, qi, v, dcol

    ahead = [independent_part(it) for it in items[:GLA_LOOKAHEAD]]
    for i, (ic, h) in enumerate(items):
        if i + GLA_LOOKAHEAD < len(items):
            ahead.append(independent_part(items[i + GLA_LOOKAHEAD]))
        sc, u, qi, v, dcol = ahead.pop(0)
        rs = slice(ic * c, (ic + 1) * c)
        vs = slice(h * dv, (h + 1) * dv)
        s_prev = s_ref[h]
        o = _dot(jnp.where(causal, sc, 0.0).astype(BF16), v) + _dot(qi, s_prev.astype(BF16))
        s_ref[h] = jnp.concatenate([dcol] * (dv // dk), axis=1) * s_prev + u
        r = r_ref[rs, vs].astype(F32)
        o_ref[rs, vs] = (_rms(o, gn) * (r * jax.nn.sigmoid(r))).astype(BF16)


def _gla(q, k, v, la, r, out_gain, batch, seq):
    dqk = q.shape[1]
    dvt = v.shape[1]
    dk = dqk // GLA_HEADS
    dv = dvt // GLA_HEADS
    rows = min(GLA_ROWS, seq)
    nb = seq // rows
    row = lambda n: pl.BlockSpec((rows, n), lambda b, i: (b * nb + i, 0))
    return pl.pallas_call(
        functools.partial(_gla_body, rows=rows, dk=dk, dv=dv),
        grid=(batch, nb),
        in_specs=[row(dqk), row(dqk), row(dvt), row(dqk), row(dvt), _layer_spec(*out_gain)],
        out_specs=row(dvt),
        out_shape=jax.ShapeDtypeStruct((batch * seq, dvt), BF16),
        scratch_shapes=[pltpu.VMEM((GLA_HEADS, dk, dv), F32)],
        compiler_params=_params("parallel", "arbitrary"),
        name="gla_scan",
    )(q, k, v, la, r, out_gain[0])


def _rope(x, cos, sin):
    return x * cos + pltpu.roll(x, LANES // 2, axis=1) * sin


def _mla_in_body(x_ref, g_ref, w_ref, qn_ref, kvn_ref, wuq_ref, wuk_ref, wuvt_ref, qg_ref, kg_ref, krg_ref, mqg_ref,
                 cos_ref, sin_ref, q_out, k_out, vt_out, qm_out):
    h = _rms(x_ref[...], g_ref[...]).astype(BF16)
    p = _dot(h, w_ref[...])
    o1 = MLA_Q_RANK
    o2 = o1 + MLA_KV_RANK
    o3 = o2 + LANES
    cq = _rms(p[:, :o1], qn_ref[...]).astype(BF16)
    ckv = _rms(p[:, o1:o2], kvn_ref[...]).astype(BF16)
    qm_out[...] = _mem_q(p[:, o3:], mqg_ref[...])
    cos = cos_ref[...]
    sin = sin_ref[...]
    k_rope = _rope(_rms(p[:, o2:o3], krg_ref[...], denom=MLA_ROPE), cos, sin).astype(BF16)
    q = _dot(cq, wuq_ref[...])
    kn_all = _dot(ckv, wuk_ref[...])
    vt_out[...] = _dot_nt(wuvt_ref[...], ckv).astype(BF16)
    qg = qg_ref[...]
    for hd in range(MLA_HEADS):
        b0 = hd * MLA_QK_PAD
        qn = _rms(q[:, b0:b0 + LANES], qg[:, :LANES])
        qr = _rope(_rms(q[:, b0 + LANES:b0 + MLA_QK_PAD], qg[:, LANES:], denom=MLA_ROPE), cos, sin)
        q_out[:, b0:b0 + LANES] = qn.astype(BF16)
        q_out[:, b0 + LANES:b0 + MLA_QK_PAD] = qr.astype(BF16)
        kn = _rms(kn_all[:, hd * MLA_NOPE:(hd + 1) * MLA_NOPE], kg_ref[...])
        k_out[:, b0:b0 + LANES] = kn.astype(BF16)
        k_out[:, b0 + LANES:b0 + MLA_QK_PAD] = k_rope


def _mla_in(x, params, cos, sin, seq):
    t, d = x.shape
    tm = min(TOKEN_TILE, t, seq)
    nb = seq // tm
    row = lambda n: pl.BlockSpec((tm, n), lambda i: (i, 0))
    pos = pl.BlockSpec((tm, LANES), lambda i: (i % nb, 0))
    dq = MLA_HEADS * MLA_QK_PAD
    dvt = MLA_HEADS * MLA_DV
    dmem = MEM_HEADS * MEM_DH
    specs, ops = _layered(params)
    return pl.pallas_call(
        _mla_in_body,
        grid=(t // tm,),
        in_specs=[row(d)] + specs + [pos, pos],
        out_specs=[row(dq), row(dq), pl.BlockSpec((dvt, tm), lambda i: (0, i)), row(dmem)],
        out_shape=[jax.ShapeDtypeStruct((t, dq), BF16), jax.ShapeDtypeStruct((t, dq), BF16),
                   jax.ShapeDtypeStruct((dvt, t), BF16), jax.ShapeDtypeStruct((t, dmem), BF16)],
        compiler_params=_params("parallel"),
        name="mla_in",
    )(x, *ops, cos, sin)


def _flash_body(q_ref, k_ref, vt_ref, o_ref, *, tq, tk, rc):
    qi = pl.program_id(2)
    n_diag = tq // tk
    n_chain = tq // rc
    dv = vt_ref.shape[0]
    ones = jnp.ones((ONES_ROWS, tk), BF16)

    def run_blocks(j0, state, diagonal):
        items = []
        for d in range(n_diag):
            for c in range(n_chain):
                lo_q, hi_q = c * rc, (c + 1) * rc - 1
                lo_k, hi_k = d * tk, (d + 1) * tk - 1
                if diagonal and lo_k > hi_q:
                    continue
                masked = diagonal and hi_k > lo_q
                assert not masked or lo_k <= lo_q
                items.append((d, c, masked))
        blocks = {}

        def key_block(d):
            if d not in blocks:
                k0 = pl.multiple_of((j0 + d) * tk, tk)
                blocks[d] = (k_ref[pl.ds(k0, tk), :],
                             jnp.concatenate([vt_ref[:, pl.ds(k0, tk)], ones], axis=0))
            return blocks[d]

        def scores(item):
            d, c, _ = item
            return _dot_nt(key_block(d)[0], q_ref[c * rc:(c + 1) * rc, :])

        state = list(state)
        ahead = [scores(it) for it in items[:FLASH_LOOKAHEAD]]
        for i, (d, c, masked) in enumerate(items):
            if i + FLASH_LOOKAHEAD < len(items):
                ahead.append(scores(items[i + FLASH_LOOKAHEAD]))
            st = ahead.pop(0)
            if masked:
                kpos = d * tk + lax.broadcasted_iota(jnp.int32, (tk, rc), 0)
                qpos = c * rc + lax.broadcasted_iota(jnp.int32, (tk, rc), 1)
                st = jnp.where(kpos <= qpos, st, NEG_BIG)
            m, acc = state[c]
            m_new = jnp.maximum(m, jnp.max(st, axis=0, keepdims=True))
            alpha = jnp.exp2(m - m_new)
            pt = jnp.exp2(st - m_new).astype(BF16)
            state[c] = (m_new, alpha * acc + _dot(key_block(d)[1], pt))
        return tuple(state)

    init = tuple((jnp.full((1, rc), NEG_BIG, F32), jnp.zeros((dv + ONES_ROWS, rc), F32)) for _ in range(n_chain))
    state = lax.fori_loop(0, qi, lambda t, st: run_blocks(t * n_diag, st, False), init)
    state = run_blocks(qi * n_diag, state, True)
    for c in range(n_chain):
        acc = state[c][1]
        o_ref[c * rc:(c + 1) * rc, :] = (acc[:dv, :] / acc[dv:dv + 1, :]).T.astype(o_ref.dtype)


def _flash(q, k, vt, batch, seq):
    tq = min(FLASH_TQ, seq)
    tk = min(FLASH_TK, tq)
    rc = min(FLASH_CHAIN_ROWS, tq)
    q3 = q.reshape(batch, seq, -1)
    k3 = k.reshape(batch, seq, -1)
    out = pl.pallas_call(
        functools.partial(_flash_body, tq=tq, tk=tk, rc=rc),
        grid=(batch, MLA_HEADS, seq // tq),
        in_specs=[pl.BlockSpec((None, tq, MLA_QK_PAD), lambda b, h, i: (b, i, h)),
                  pl.BlockSpec((None, seq, MLA_QK_PAD), lambda b, h, i: (b, 0, h)),
                  pl.BlockSpec((MLA_DV, seq), lambda b, h, i: (h, b))],
        out_specs=pl.BlockSpec((None, tq, MLA_DV), lambda b, h, i: (b, i, h)),
        out_shape=jax.ShapeDtypeStruct((batch, seq, MLA_HEADS * MLA_DV), BF16),
        compiler_params=_params("parallel", "parallel", "arbitrary"),
        name="mla_flash",
    )(q3, k3, vt)
    return out.reshape(batch * seq, -1)


def _memkv_body(mem_ref, g_ref, w_ref, kg_ref, k_out, v_out):
    m = _rms(mem_ref[...], g_ref[...]).astype(BF16)
    kv = _dot(m, w_ref[...])
    dm = MEM_HEADS * MEM_DH
    heads = [_rms(kv[:, h * MEM_DH:(h + 1) * MEM_DH], kg_ref[...]) for h in range(MEM_HEADS)]
    k_out[...] = jnp.concatenate(heads, axis=-1).astype(BF16)
    v_out[...] = kv[:, dm:].astype(BF16)


def _memkv(mem, params):
    b, m, d = mem.shape
    specs, ops = _layered(params)
    dm = MEM_HEADS * MEM_DH
    blk = pl.BlockSpec((None, m, dm), lambda i: (i, 0, 0))
    return pl.pallas_call(
        _memkv_body,
        grid=(b,),
        in_specs=[pl.BlockSpec((None, m, d), lambda i: (i, 0, 0))] + specs,
        out_specs=[blk, blk],
        out_shape=[jax.ShapeDtypeStruct((b, m, dm), BF16)] * 2,
        compiler_params=_params("parallel"),
        name="mem_kv",
    )(mem, *ops)


def _out_body(x_ref, o_ref, qm_ref, km_ref, vm_ref, wo_ref, out_ref):
    qm = qm_ref[...]
    km = km_ref[...]
    vm = vm_ref[...]
    heads = []
    for h in range(MEM_HEADS):
        hs = slice(h * MEM_DH, (h + 1) * MEM_DH)
        s = _dot_nt(qm[:, hs], km[:, hs])
        p = jnp.exp(s - jnp.max(s, axis=-1, keepdims=True))
        l = jnp.sum(p, axis=-1, keepdims=True)
        heads.append((_dot(p.astype(BF16), vm[:, hs]) / l).astype(BF16))
    om = jnp.concatenate(heads, axis=-1)
    dmix = o_ref.shape[-1]
    y = _dot(o_ref[...], wo_ref[:dmix, :]) + _dot(om, wo_ref[dmix:, :])
    out_ref[...] = x_ref[...] + y


def _out_proj(x, o, q_mem, k_mem, v_mem, w_out, seq):
    t, d = x.shape
    tm = min(TOKEN_TILE, t, seq)
    nb = seq // tm
    row = lambda n: pl.BlockSpec((tm, n), lambda i: (i, 0))
    mem_blk = pl.BlockSpec((None,) + k_mem.shape[1:], lambda i: (i // nb, 0, 0))
    return pl.pallas_call(
        _out_body,
        grid=(t // tm,),
        in_specs=[row(d), row(o.shape[1]), row(q_mem.shape[1]), mem_blk, mem_blk, _layer_spec(*w_out)],
        out_specs=row(d),
        out_shape=jax.ShapeDtypeStruct((t, d), F32),
        compiler_params=_params("parallel"),
        name="out_proj",
    )(x, o, q_mem, k_mem, v_mem, w_out[0])


def _pad_rope(w):
    half = MLA_ROPE // 2
    z = jnp.zeros(w.shape[:-1] + (LANES // 2 - half,), w.dtype)
    return jnp.concatenate([w[..., :half], z, w[..., half:], z], axis=-1)


def _rope_tables(seq):
    inv = 1.0 / (ROPE_BASE ** (jnp.arange(0, MLA_ROPE, 2, dtype=F32) / MLA_ROPE))
    ang = jnp.arange(seq, dtype=F32)[:, None] * inv[None, :]
    cos = _pad_rope(jnp.concatenate([jnp.cos(ang)] * 2, axis=-1))
    sin = _pad_rope(jnp.concatenate([-jnp.sin(ang), jnp.sin(ang)], axis=-1))
    return cos, sin


def kernel(x, mem, ffn1_norm, ffn1_w_gate, ffn1_w_up, ffn1_w_down, ffn2_norm, ffn2_w_gate, ffn2_w_up, ffn2_w_down, mix_norm, w_out, mem_norm, mem_w_kv, memq_norm, memk_norm, gla_w_in, gla_w_alpha, gla_b_alpha, gla_out_norm, mla_w_in, mla_q_norm, mla_kv_norm, mla_w_uq, mla_w_ukv, mla_qn_norm, mla_qr_norm, mla_kn_norm, mla_kr_norm):
    batch, seq, d = x.shape
    depth = ffn1_norm.shape[0]
    t = batch * seq
    xs = x.reshape(t, d)
    cos, sin = _rope_tables(seq)
    gains = lambda a: a.astype(F32)[:, None, :]
    bf = lambda a: a.astype(BF16)

    ffn1 = [gains(ffn1_norm), bf(ffn1_w_gate), bf(ffn1_w_up), bf(ffn1_w_down)]
    ffn2 = [gains(ffn2_norm), bf(ffn2_w_gate), bf(ffn2_w_up), bf(ffn2_w_down)]
    mix_g, memq_g = gains(mix_norm), gains(memq_norm)
    memkv = [gains(mem_norm), bf(mem_w_kv), gains(memk_norm)]
    w_out_b = bf(w_out)

    gla_dqk = gla_w_alpha.shape[-1]
    gla_dv = d
    dmem = MEM_HEADS * MEM_DH
    a_lo = 2 * gla_dqk + gla_dv
    a_hi = a_lo + GLA_GATE_RANK
    pad_rank = LANES - GLA_GATE_RANK
    gla_w = bf(jnp.concatenate([gla_w_in[..., :a_lo], gla_w_in[..., a_hi:],
                                jnp.pad(gla_w_in[..., a_lo:a_hi], ((0, 0), (0, 0), (0, pad_rank)))], axis=-1))
    gla_wa = bf(jnp.pad(gla_w_alpha, ((0, 0), (0, pad_rank), (0, 0))))
    gla_ba, gla_og = gains(gla_b_alpha), gains(gla_out_norm)

    r_lo = MLA_Q_RANK + MLA_KV_RANK
    r_hi = r_lo + MLA_ROPE
    n_mla = mla_w_in.shape[0]
    mla_w = bf(jnp.concatenate([mla_w_in[..., :r_lo], _pad_rope(mla_w_in[..., r_lo:r_hi]),
                                mla_w_in[..., r_hi:]], axis=-1))
    uq = mla_w_uq.reshape(n_mla, MLA_Q_RANK, MLA_HEADS, MLA_NOPE + MLA_ROPE)
    w_uq = bf(jnp.concatenate([uq[..., :MLA_NOPE], _pad_rope(uq[..., MLA_NOPE:])], axis=-1)
              .reshape(n_mla, MLA_Q_RANK, MLA_HEADS * MLA_QK_PAD))
    ukv = mla_w_ukv.reshape(n_mla, MLA_KV_RANK, MLA_HEADS, MLA_NOPE + MLA_DV)
    w_uk = bf(ukv[..., :MLA_NOPE].reshape(n_mla, MLA_KV_RANK, -1))
    w_uvt = bf(jnp.swapaxes(ukv[..., MLA_NOPE:].reshape(n_mla, MLA_KV_RANK, -1), 1, 2))
    qk_scale = (MLA_NOPE + MLA_ROPE) ** -0.5 * LOG2_E
    mla_qg = gains(jnp.concatenate([mla_qn_norm, _pad_rope(mla_qr_norm)], axis=-1) * qk_scale)
    mla_qn, mla_kvn = gains(mla_q_norm), gains(mla_kv_norm)
    mla_kg, mla_krg = gains(mla_kn_norm), gains(_pad_rope(mla_kr_norm))

    for i in range(depth):
        j = i // 2
        xs = _ffn(xs, [(a, i) for a in ffn1])
        k_mem, v_mem = _memkv(mem, [(a, i) for a in memkv])
        if i % 2 == 0:
            q, k, v, r, q_mem, la = _gla_in(xs, [(mix_g, i), (gla_w, j), (gla_wa, j), (gla_ba, j), (memq_g, i)],
                                            gla_dqk, gla_dv, dmem)
            o = _gla(q, k, v, la, r, (gla_og, j), batch, seq)
        else:
            q, k, vt, q_mem = _mla_in(xs, [(mix_g, i), (mla_w, j), (mla_qn, j), (mla_kvn, j), (w_uq, j), (w_uk, j),
                                           (w_uvt, j), (mla_qg, j), (mla_kg, j), (mla_krg, j), (memq_g, i)],
                                      cos, sin, seq)
            o = _flash(q, k, vt, batch, seq)
        xs = _out_proj(xs, o, q_mem, k_mem, v_mem, (w_out_b, i), seq)
        xs = _ffn(xs, [(a, i) for a in ffn2])
    return xs.reshape(batch, seq, d)
```

```python
import functools

import jax
import jax.numpy as jnp
from jax import lax
from jax.experimental import pallas as pl
from jax.experimental.pallas import tpu as pltpu

F32 = jnp.float32
BF16 = jnp.bfloat16

EPS = 1e-6
GLA_HEADS = 4
GLA_GATE_RANK = 16
GLA_GATE_TAU = 16.0
GLA_CHUNK = 64
MLA_HEADS = 8
MLA_Q_RANK = 384
MLA_KV_RANK = 256
MLA_NOPE = 128
MLA_ROPE = 64
MLA_DV = 128
ROPE_BASE = 10000.0
MEM_HEADS = 4
MEM_DH = 128

LANES = 128
MLA_QK_PAD = 2 * LANES
VMEM_LIMIT_BYTES = 56 * 1024 * 1024

TOKEN_TILE = 512
SUBTILE_ROWS = 256
FLASH_TQ = 2048
FLASH_TK = 256
FLASH_CHAIN_ROWS = 256
FLASH_LOOKAHEAD = 16
ONES_ROWS = 16
GLA_ROWS = 512
GLA_LOOKAHEAD = 4

NEG_BIG = -1e30
LOG2_E = 1.4426950408889634


def _params(*sem):
    return pltpu.CompilerParams(dimension_semantics=sem, vmem_limit_bytes=VMEM_LIMIT_BYTES)


def _layer_spec(stacked, layer):
    nd = stacked.ndim - 1
    return pl.BlockSpec((None,) + stacked.shape[1:], lambda *_: (layer,) + (0,) * nd,
                        pipeline_mode=pl.Buffered(1))


def _layered(pairs):
    return [_layer_spec(a, l) for a, l in pairs], [a for a, _ in pairs]


def _rms(x, gain, denom=None):
    d = x.shape[-1] if denom is None else denom
    ms = jnp.sum(x * x, axis=-1, keepdims=True) * (1.0 / d)
    return x * lax.rsqrt(ms + EPS) * gain


def _dot(a, b):
    return jnp.dot(a, b, preferred_element_type=F32)


def _dot_nt(a, b):
    return lax.dot_general(a, b, (((1,), (1,)), ((), ())), preferred_element_type=F32)


def _dot_tn(a, b):
    return lax.dot_general(a, b, (((0,), (0,)), ((), ())), preferred_element_type=F32)


def _pipelined_subtiles(rows, matmuls, epilogue):
    sub = min(SUBTILE_ROWS, rows)
    slices = [slice(s * sub, (s + 1) * sub) for s in range(rows // sub)]
    staged = matmuls(slices[0])
    for s, rs in enumerate(slices):
        ready = staged
        if s + 1 < len(slices):
            staged = matmuls(slices[s + 1])
        epilogue(rs, *ready)


def _ffn_body(x_ref, g_ref, wg_ref, wu_ref, wd_ref, o_ref):
    def matmuls(rs):
        h = _rms(x_ref[rs, :], g_ref[...]).astype(BF16)
        return _dot(h, wg_ref[...]), _dot(h, wu_ref[...])

    def epilogue(rs, gate, up):
        a = (gate * jax.nn.sigmoid(gate) * up).astype(BF16)
        o_ref[rs, :] = x_ref[rs, :] + 0.5 * _dot(a, wd_ref[...])

    _pipelined_subtiles(x_ref.shape[0], matmuls, epilogue)


def _ffn(x, params):
    t, d = x.shape
    tm = min(TOKEN_TILE, t)
    row = pl.BlockSpec((tm, d), lambda i: (i, 0))
    specs, ops = _layered(params)
    return pl.pallas_call(
        _ffn_body,
        grid=(t // tm,),
        in_specs=[row] + specs,
        out_specs=row,
        out_shape=jax.ShapeDtypeStruct((t, d), F32),
        compiler_params=_params("parallel"),
        name="ffn",
    )(x, *ops)


def _mem_q(p, gain):
    heads = [_rms(p[:, h * MEM_DH:(h + 1) * MEM_DH], gain) * (MEM_DH ** -0.5) for h in range(MEM_HEADS)]
    return jnp.concatenate(heads, axis=-1).astype(BF16)


def _gla_in_body(x_ref, g_ref, w_ref, wa_ref, ba_ref, mqg_ref,
                 q_ref, k_ref, v_ref, r_ref, qm_ref, la_ref, *, dqk, dv, dmem):
    def matmuls(rs):
        h = _rms(x_ref[rs, :], g_ref[...]).astype(BF16)
        p = _dot(h, w_ref[...])
        o = 2 * dqk + 2 * dv + dmem
        return p, _dot(p[:, o:o + LANES].astype(BF16), wa_ref[...])

    def epilogue(rs, p, z):
        o = 0
        q_ref[rs, :] = p[:, o:o + dqk].astype(BF16); o += dqk
        k_ref[rs, :] = p[:, o:o + dqk].astype(BF16); o += dqk
        v_ref[rs, :] = p[:, o:o + dv].astype(BF16); o += dv
        r_ref[rs, :] = p[:, o:o + dv].astype(BF16); o += dv
        qm_ref[rs, :] = _mem_q(p[:, o:o + dmem], mqg_ref[...])
        z = z + ba_ref[...]
        log_sig = jnp.minimum(z, 0.0) - jnp.log1p(jnp.exp(-jnp.abs(z)))
        la_ref[rs, :] = log_sig * (1.0 / GLA_GATE_TAU)

    _pipelined_subtiles(x_ref.shape[0], matmuls, epilogue)


def _gla_in(x, params, dqk, dv, dmem):
    t, d = x.shape
    tm = min(TOKEN_TILE, t)
    row = lambda n: pl.BlockSpec((tm, n), lambda i: (i, 0))
    outs = [(dqk, BF16), (dqk, BF16), (dv, BF16), (dv, BF16), (dmem, BF16), (dqk, F32)]
    specs, ops = _layered(params)
    return pl.pallas_call(
        functools.partial(_gla_in_body, dqk=dqk, dv=dv, dmem=dmem),
        grid=(t // tm,),
        in_specs=[row(d)] + specs,
        out_specs=[row(n) for n, _ in outs],
        out_shape=[jax.ShapeDtypeStruct((t, n), dt) for n, dt in outs],
        compiler_params=_params("parallel"),
        name="gla_in",
    )(x, *ops)


def _gla_body(q_ref, k_ref, v_ref, la_ref, r_ref, gn_ref, o_ref, s_ref, *, rows, dk, dv):
    c = GLA_CHUNK

    @pl.when(pl.program_id(1) == 0)
    def _():
        s_ref[...] = jnp.zeros_like(s_ref)

    ri = lax.broadcasted_iota(jnp.int32, (c, c), 0)
    ci = lax.broadcasted_iota(jnp.int32, (c, c), 1)
    causal = ci <= ri
    tril = causal.astype(F32)
    gn = gn_ref[...]

    items = [(ic, h) for ic in range(rows // c) for h in range(GLA_HEADS)]
    prefix = {}

    def cum_decay(ic):
        if ic not in prefix:
            prefix[ic] = jnp.dot(tril, la_ref[ic * c:(ic + 1) * c, :], precision=lax.Precision.HIGHEST,
                                 preferred_element_type=F32)
        return prefix[ic]

    def independent_part(item):
        ic, h = item
        rs = slice(ic * c, (ic + 1) * c)
        ks = slice(h * dk, (h + 1) * dk)
        bc = cum_decay(ic)[:, ks]
        b_last = bc[c - 1:c, :]
        b_mid = bc[c // 2 - 1:c // 2, :]
        q = q_ref[rs, ks].astype(F32) * (dk ** -0.5)
        k = k_ref[rs, ks].astype(F32)
        v = v_ref[rs, h * dv:(h + 1) * dv]
        qe = (q * jnp.exp(bc - b_mid)).astype(BF16)
        ke = (k * jnp.exp(b_mid - bc)).astype(BF16)
        sc = _dot_nt(qe, ke)
        u = _dot_tn((k * jnp.exp(b_last - bc)).astype(BF16), v)
        qi = (q * jnp.exp(bc)).astype(BF16)
        dcol = jnp.exp(jnp.broadcast_to(b_last, (dk, dk))).T
        return sc, u, qi, v, dcol

    ahead = [independent_part(it) for it in items[:GLA_LOOKAHEAD]]
    for i, (ic, h) in enumerate(items):
        if i + GLA_LOOKAHEAD < len(items):
            ahead.append(independent_part(items[i + GLA_LOOKAHEAD]))
        sc, u, qi, v, dcol = ahead.pop(0)
        rs = slice(ic * c, (ic + 1) * c)
        vs = slice(h * dv, (h + 1) * dv)
        s_prev = s_ref[h]
        o = _dot(jnp.where(causal, sc, 0.0).astype(BF16), v) + _dot(qi, s_prev.astype(BF16))
        s_ref[h] = jnp.concatenate([dcol] * (dv // dk), axis=1) * s_prev + u
        r = r_ref[rs, vs].astype(F32)
        o_ref[rs, vs] = (_rms(o, gn) * (r * jax.nn.sigmoid(r))).astype(BF16)


def _gla(q, k, v, la, r, out_gain, batch, seq):
    dqk = q.shape[1]
    dvt = v.shape[1]
    dk = dqk // GLA_HEADS
    dv = dvt // GLA_HEADS
    rows = min(GLA_ROWS, seq)
    nb = seq // rows
    row = lambda n: pl.BlockSpec((rows, n), lambda b, i: (b * nb + i, 0))
    return pl.pallas_call(
        functools.partial(_gla_body, rows=rows, dk=dk, dv=dv),
        grid=(batch, nb),
        in_specs=[row(dqk), row(dqk), row(dvt), row(dqk), row(dvt), _layer_spec(*out_gain)],
        out_specs=row(dvt),
        out_shape=jax.ShapeDtypeStruct((batch * seq, dvt), BF16),
        scratch_shapes=[pltpu.VMEM((GLA_HEADS, dk, dv), F32)],
        compiler_params=_params("parallel", "arbitrary"),
        name="gla_scan",
    )(q, k, v, la, r, out_gain[0])


def _rope(x, cos, sin):
    return x * cos + pltpu.roll(x, LANES // 2, axis=1) * sin


def _mla_in_body(x_ref, g_ref, w_ref, qn_ref, kvn_ref, wuq_ref, wuk_ref, wuvt_ref, qg_ref, kg_ref, krg_ref, mqg_ref,
                 cos_ref, sin_ref, q_out, k_out, vt_out, qm_out):
    o1 = MLA_Q_RANK
    o2 = o1 + MLA_KV_RANK
    qg = qg_ref[...]

    def matmuls(rs):
        h = _rms(x_ref[rs, :], g_ref[...]).astype(BF16)
        p = _dot(h, w_ref[...])
        cq = _rms(p[:, :o1], qn_ref[...]).astype(BF16)
        ckv = _rms(p[:, o1:o2], kvn_ref[...]).astype(BF16)
        return p[:, o2:], _dot(cq, wuq_ref[...]), _dot(ckv, wuk_ref[...]), _dot_nt(wuvt_ref[...], ckv)

    def epilogue(rs, p_tail, q, kn_all, vt):
        vt_out[:, rs] = vt.astype(BF16)
        qm_out[rs, :] = _mem_q(p_tail[:, LANES:], mqg_ref[...])
        cos = cos_ref[rs, :]
        sin = sin_ref[rs, :]
        k_rope = _rope(_rms(p_tail[:, :LANES], krg_ref[...], denom=MLA_ROPE), cos, sin).astype(BF16)
        for hd in range(MLA_HEADS):
            b0 = hd * MLA_QK_PAD
            qn = _rms(q[:, b0:b0 + LANES], qg[:, :LANES])
            qr = _rope(_rms(q[:, b0 + LANES:b0 + MLA_QK_PAD], qg[:, LANES:], denom=MLA_ROPE), cos, sin)
            q_out[rs, b0:b0 + LANES] = qn.astype(BF16)
            q_out[rs, b0 + LANES:b0 + MLA_QK_PAD] = qr.astype(BF16)
            kn = _rms(kn_all[:, hd * MLA_NOPE:(hd + 1) * MLA_NOPE], kg_ref[...])
            k_out[rs, b0:b0 + LANES] = kn.astype(BF16)
            k_out[rs, b0 + LANES:b0 + MLA_QK_PAD] = k_rope

    _pipelined_subtiles(x_ref.shape[0], matmuls, epilogue)


def _mla_in(x, params, cos, sin, seq):
    t, d = x.shape
    tm = min(TOKEN_TILE, t, seq)
    nb = seq // tm
    row = lambda n: pl.BlockSpec((tm, n), lambda i: (i, 0))
    pos = pl.BlockSpec((tm, LANES), lambda i: (i % nb, 0))
    dq = MLA_HEADS * MLA_QK_PAD
    dvt = MLA_HEADS * MLA_DV
    dmem = MEM_HEADS * MEM_DH
    specs, ops = _layered(params)
    return pl.pallas_call(
        _mla_in_body,
        grid=(t // tm,),
        in_specs=[row(d)] + specs + [pos, pos],
        out_specs=[row(dq), row(dq), pl.BlockSpec((dvt, tm), lambda i: (0, i)), row(dmem)],
        out_shape=[jax.ShapeDtypeStruct((t, dq), BF16), jax.ShapeDtypeStruct((t, dq), BF16),
                   jax.ShapeDtypeStruct((dvt, t), BF16), jax.ShapeDtypeStruct((t, dmem), BF16)],
        compiler_params=_params("parallel"),
        name="mla_in",
    )(x, *ops, cos, sin)


def _flash_body(q_ref, k_ref, vt_ref, o_ref, *, tq, tk, rc):
    qi = pl.program_id(2)
    n_diag = tq // tk
    n_chain = tq // rc
    dv = vt_ref.shape[0]
    ones = jnp.ones((ONES_ROWS, tk), BF16)

    def run_blocks(j0, state, diagonal):
        items = []
        for d in range(n_diag):
            for c in range(n_chain):
                lo_q, hi_q = c * rc, (c + 1) * rc - 1
                lo_k, hi_k = d * tk, (d + 1) * tk - 1
                if diagonal and lo_k > hi_q:
                    continue
                masked = diagonal and hi_k > lo_q
                assert not masked or lo_k <= lo_q
                items.append((d, c, masked))
        blocks = {}

        def key_block(d):
            if d not in blocks:
                k0 = pl.multiple_of((j0 + d) * tk, tk)
                blocks[d] = (k_ref[pl.ds(k0, tk), :],
                             jnp.concatenate([vt_ref[:, pl.ds(k0, tk)], ones], axis=0))
            return blocks[d]

        def scores(item):
            d, c, _ = item
            return _dot_nt(key_block(d)[0], q_ref[c * rc:(c + 1) * rc, :])

        state = list(state)
        ahead = [scores(it) for it in items[:FLASH_LOOKAHEAD]]
        for i, (d, c, masked) in enumerate(items):
            if i + FLASH_LOOKAHEAD < len(items):
                ahead.append(scores(items[i + FLASH_LOOKAHEAD]))
            st = ahead.pop(0)
            if masked:
                kpos = d * tk + lax.broadcasted_iota(jnp.int32, (tk, rc), 0)
                qpos = c * rc + lax.broadcasted_iota(jnp.int32, (tk, rc), 1)
                st = jnp.where(kpos <= qpos, st, NEG_BIG)
            m, acc = state[c]
            m_new = jnp.maximum(m, jnp.max(st, axis=0, keepdims=True))
            alpha = jnp.exp2(m - m_new)
            pt = jnp.exp2(st - m_new).astype(BF16)
            state[c] = (m_new, alpha * acc + _dot(key_block(d)[1], pt))
        return tuple(state)

    init = tuple((jnp.full((1, rc), NEG_BIG, F32), jnp.zeros((dv + ONES_ROWS, rc), F32)) for _ in range(n_chain))
    state = lax.fori_loop(0, qi, lambda t, st: run_blocks(t * n_diag, st, False), init)
    state = run_blocks(qi * n_diag, state, True)
    for c in range(n_chain):
        acc = state[c][1]
        o_ref[c * rc:(c + 1) * rc, :] = (acc[:dv, :] / acc[dv:dv + 1, :]).T.astype(o_ref.dtype)


def _flash(q, k, vt, batch, seq):
    tq = min(FLASH_TQ, seq)
    tk = min(FLASH_TK, tq)
    rc = min(FLASH_CHAIN_ROWS, tq)
    q3 = q.reshape(batch, seq, -1)
    k3 = k.reshape(batch, seq, -1)
    out = pl.pallas_call(
        functools.partial(_flash_body, tq=tq, tk=tk, rc=rc),
        grid=(batch, MLA_HEADS, seq // tq),
        in_specs=[pl.BlockSpec((None, tq, MLA_QK_PAD), lambda b, h, i: (b, i, h)),
                  pl.BlockSpec((None, seq, MLA_QK_PAD), lambda b, h, i: (b, 0, h)),
                  pl.BlockSpec((MLA_DV, seq), lambda b, h, i: (h, b))],
        out_specs=pl.BlockSpec((None, tq, MLA_DV), lambda b, h, i: (b, i, h)),
        out_shape=jax.ShapeDtypeStruct((batch, seq, MLA_HEADS * MLA_DV), BF16),
        compiler_params=_params("parallel", "parallel", "arbitrary"),
        name="mla_flash",
    )(q3, k3, vt)
    return out.reshape(batch * seq, -1)


def _memkv_body(mem_ref, g_ref, w_ref, kg_ref, k_out, v_out):
    m = _rms(mem_ref[...], g_ref[...]).astype(BF16)
    kv = _dot(m, w_ref[...])
    dm = MEM_HEADS * MEM_DH
    heads = [_rms(kv[:, h * MEM_DH:(h + 1) * MEM_DH], kg_ref[...]) for h in range(MEM_HEADS)]
    k_out[...] = jnp.concatenate(heads, axis=-1).astype(BF16)
    v_out[...] = kv[:, dm:].astype(BF16)


def _memkv(mem, params):
    b, m, d = mem.shape
    specs, ops = _layered(params)
    dm = MEM_HEADS * MEM_DH
    blk = pl.BlockSpec((None, m, dm), lambda i: (i, 0, 0))
    return pl.pallas_call(
        _memkv_body,
        grid=(b,),
        in_specs=[pl.BlockSpec((None, m, d), lambda i: (i, 0, 0))] + specs,
        out_specs=[blk, blk],
        out_shape=[jax.ShapeDtypeStruct((b, m, dm), BF16)] * 2,
        compiler_params=_params("parallel"),
        name="mem_kv",
    )(mem, *ops)


def _out_body(x_ref, o_ref, qm_ref, km_ref, vm_ref, wo_ref, out_ref):
    qm = qm_ref[...]
    km = km_ref[...]
    vm = vm_ref[...]
    heads = []
    for h in range(MEM_HEADS):
        hs = slice(h * MEM_DH, (h + 1) * MEM_DH)
        s = _dot_nt(qm[:, hs], km[:, hs])
        p = jnp.exp(s - jnp.max(s, axis=-1, keepdims=True))
        l = jnp.sum(p, axis=-1, keepdims=True)
        heads.append((_dot(p.astype(BF16), vm[:, hs]) / l).astype(BF16))
    om = jnp.concatenate(heads, axis=-1)
    dmix = o_ref.shape[-1]
    y = _dot(o_ref[...], wo_ref[:dmix, :]) + _dot(om, wo_ref[dmix:, :])
    out_ref[...] = x_ref[...] + y


def _out_proj(x, o, q_mem, k_mem, v_mem, w_out, seq):
    t, d = x.shape
    tm = min(TOKEN_TILE, t, seq)
    nb = seq // tm
    row = lambda n: pl.BlockSpec((tm, n), lambda i: (i, 0))
    mem_blk = pl.BlockSpec((None,) + k_mem.shape[1:], lambda i: (i // nb, 0, 0))
    return pl.pallas_call(
        _out_body,
        grid=(t // tm,),
        in_specs=[row(d), row(o.shape[1]), row(q_mem.shape[1]), mem_blk, mem_blk, _layer_spec(*w_out)],
        out_specs=row(d),
        out_shape=jax.ShapeDtypeStruct((t, d), F32),
        compiler_params=_params("parallel"),
        name="out_proj",
    )(x, o, q_mem, k_mem, v_mem, w_out[0])


def _pad_rope(w):
    half = MLA_ROPE // 2
    z = jnp.zeros(w.shape[:-1] + (LANES // 2 - half,), w.dtype)
    return jnp.concatenate([w[..., :half], z, w[..., half:], z], axis=-1)


def _rope_tables(seq):
    inv = 1.0 / (ROPE_BASE ** (jnp.arange(0, MLA_ROPE, 2, dtype=F32) / MLA_ROPE))
    ang = jnp.arange(seq, dtype=F32)[:, None] * inv[None, :]
    cos = _pad_rope(jnp.concatenate([jnp.cos(ang)] * 2, axis=-1))
    sin = _pad_rope(jnp.concatenate([-jnp.sin(ang), jnp.sin(ang)], axis=-1))
    return cos, sin


def kernel(x, mem, ffn1_norm, ffn1_w_gate, ffn1_w_up, ffn1_w_down, ffn2_norm, ffn2_w_gate, ffn2_w_up, ffn2_w_down, mix_norm, w_out, mem_norm, mem_w_kv, memq_norm, memk_norm, gla_w_in, gla_w_alpha, gla_b_alpha, gla_out_norm, mla_w_in, mla_q_norm, mla_kv_norm, mla_w_uq, mla_w_ukv, mla_qn_norm, mla_qr_norm, mla_kn_norm, mla_kr_norm):
    batch, seq, d = x.shape
    depth = ffn1_norm.shape[0]
    t = batch * seq
    xs = x.reshape(t, d)
    cos, sin = _rope_tables(seq)
    gains = lambda a: a.astype(F32)[:, None, :]
    bf = lambda a: a.astype(BF16)

    ffn1 = [gains(ffn1_norm), bf(ffn1_w_gate), bf(ffn1_w_up), bf(ffn1_w_down)]
    ffn2 = [gains(ffn2_norm), bf(ffn2_w_gate), bf(ffn2_w_up), bf(ffn2_w_down)]
    mix_g, memq_g = gains(mix_norm), gains(memq_norm)
    memkv = [gains(mem_norm), bf(mem_w_kv), gains(memk_norm)]
    w_out_b = bf(w_out)

    gla_dqk = gla_w_alpha.shape[-1]
    gla_dv = d
    dmem = MEM_HEADS * MEM_DH
    a_lo = 2 * gla_dqk + gla_dv
    a_hi = a_lo + GLA_GATE_RANK
    pad_rank = LANES - GLA_GATE_RANK
    gla_w = bf(jnp.concatenate([gla_w_in[..., :a_lo], gla_w_in[..., a_hi:],
                                jnp.pad(gla_w_in[..., a_lo:a_hi], ((0, 0), (0, 0), (0, pad_rank)))], axis=-1))
    gla_wa = bf(jnp.pad(gla_w_alpha, ((0, 0), (0, pad_rank), (0, 0))))
    gla_ba, gla_og = gains(gla_b_alpha), gains(gla_out_norm)

    r_lo = MLA_Q_RANK + MLA_KV_RANK
    r_hi = r_lo + MLA_ROPE
    n_mla = mla_w_in.shape[0]
    mla_w = bf(jnp.concatenate([mla_w_in[..., :r_lo], _pad_rope(mla_w_in[..., r_lo:r_hi]),
                                mla_w_in[..., r_hi:]], axis=-1))
    uq = mla_w_uq.reshape(n_mla, MLA_Q_RANK, MLA_HEADS, MLA_NOPE + MLA_ROPE)
    w_uq = bf(jnp.concatenate([uq[..., :MLA_NOPE], _pad_rope(uq[..., MLA_NOPE:])], axis=-1)
              .reshape(n_mla, MLA_Q_RANK, MLA_HEADS * MLA_QK_PAD))
    ukv = mla_w_ukv.reshape(n_mla, MLA_KV_RANK, MLA_HEADS, MLA_NOPE + MLA_DV)
    w_uk = bf(ukv[..., :MLA_NOPE].reshape(n_mla, MLA_KV_RANK, -1))
    w_uvt = bf(jnp.swapaxes(ukv[..., MLA_NOPE:].reshape(n_mla, MLA_KV_RANK, -1), 1, 2))
    qk_scale = (MLA_NOPE + MLA_ROPE) ** -0.5 * LOG2_E
    mla_qg = gains(jnp.concatenate([mla_qn_norm, _pad_rope(mla_qr_norm)], axis=-1) * qk_scale)
    mla_qn, mla_kvn = gains(mla_q_norm), gains(mla_kv_norm)
    mla_kg, mla_krg = gains(mla_kn_norm), gains(_pad_rope(mla_kr_norm))

    for i in range(depth):
        j = i // 2
        xs = _ffn(xs, [(a, i) for a in ffn1])
        k_mem, v_mem = _memkv(mem, [(a, i) for a in memkv])
        if i % 2 == 0:
            q, k, v, r, q_mem, la = _gla_in(xs, [(mix_g, i), (gla_w, j), (gla_wa, j), (gla_ba, j), (memq_g, i)],
                                            gla_dqk, gla_dv, dmem)
            o = _gla(q, k, v, la, r, (gla_og, j), batch, seq)
        else:
            q, k, vt, q_mem = _mla_in(xs, [(mix_g, i), (mla_w, j), (mla_qn, j), (mla_kvn, j), (w_uq, j), (w_uk, j),
                                           (w_uvt, j), (mla_qg, j), (mla_kg, j), (mla_krg, j), (memq_g, i)],
                                      cos, sin, seq)
            o = _flash(q, k, vt, batch, seq)
        xs = _out_proj(xs, o, q_mem, k_mem, v_mem, (w_out_b, i), seq)
        xs = _ffn(xs, [(a, i) for a in ffn2])
    return xs.reshape(batch, seq, d)
```

```python
import functools

import jax
import jax.numpy as jnp
from jax import lax
from jax.experimental import pallas as pl
from jax.experimental.pallas import tpu as pltpu

F32 = jnp.float32
BF16 = jnp.bfloat16

EPS = 1e-6
GLA_HEADS = 4
GLA_GATE_RANK = 16
GLA_GATE_TAU = 16.0
GLA_CHUNK = 64
MLA_HEADS = 8
MLA_Q_RANK = 384
MLA_KV_RANK = 256
MLA_NOPE = 128
MLA_ROPE = 64
MLA_DV = 128
ROPE_BASE = 10000.0
MEM_HEADS = 4
MEM_DH = 128

LANES = 128
MLA_QK_PAD = 2 * LANES
VMEM_LIMIT_BYTES = 56 * 1024 * 1024

TOKEN_TILE = 512
SUBTILE_ROWS = 256
FLASH_TQ = 2048
FLASH_TK = 256
FLASH_CHAIN_ROWS = 256
FLASH_LOOKAHEAD = 16
ONES_ROWS = 16
GLA_LOOKAHEAD = 4

NEG_BIG = -1e30
LOG2_E = 1.4426950408889634


def _params(*sem):
    return pltpu.CompilerParams(dimension_semantics=sem, vmem_limit_bytes=VMEM_LIMIT_BYTES)


def _layer_spec(stacked, layer):
    nd = stacked.ndim - 1
    return pl.BlockSpec((None,) + stacked.shape[1:], lambda *_: (layer,) + (0,) * nd,
                        pipeline_mode=pl.Buffered(1))


def _layered(pairs):
    return [_layer_spec(a, l) for a, l in pairs], [a for a, _ in pairs]


def _rms(x, gain, denom=None):
    d = x.shape[-1] if denom is None else denom
    ms = jnp.sum(x * x, axis=-1, keepdims=True) * (1.0 / d)
    return x * lax.rsqrt(ms + EPS) * gain


def _dot(a, b):
    return jnp.dot(a, b, preferred_element_type=F32)


def _dot_nt(a, b):
    return lax.dot_general(a, b, (((1,), (1,)), ((), ())), preferred_element_type=F32)


def _dot_tn(a, b):
    return lax.dot_general(a, b, (((0,), (0,)), ((), ())), preferred_element_type=F32)


def _pipelined_subtiles(rows, matmuls, epilogue):
    sub = min(SUBTILE_ROWS, rows)
    slices = [slice(s * sub, (s + 1) * sub) for s in range(rows // sub)]
    staged = matmuls(slices[0])
    for s, rs in enumerate(slices):
        ready = staged
        if s + 1 < len(slices):
            staged = matmuls(slices[s + 1])
        epilogue(rs, *ready)


def _interleave(major, minor):
    per = -(-len(minor) // max(len(major), 1))
    for i, step in enumerate(major):
        step()
        for sub in minor[i * per:(i + 1) * per]:
            sub()
    for sub in minor[len(major) * per:]:
        sub()


def _ffn_body(x_ref, g_ref, wg_ref, wu_ref, wd_ref, o_ref):
    def matmuls(rs):
        h = _rms(x_ref[rs, :], g_ref[...]).astype(BF16)
        return _dot(h, wg_ref[...]), _dot(h, wu_ref[...])

    def epilogue(rs, gate, up):
        a = (gate * jax.nn.sigmoid(gate) * up).astype(BF16)
        o_ref[rs, :] = x_ref[rs, :] + 0.5 * _dot(a, wd_ref[...])

    _pipelined_subtiles(x_ref.shape[0], matmuls, epilogue)


def _ffn(x, params):
    t, d = x.shape
    tm = min(TOKEN_TILE, t)
    row = pl.BlockSpec((tm, d), lambda i: (i, 0))
    specs, ops = _layered(params)
    return pl.pallas_call(
        _ffn_body,
        grid=(t // tm,),
        in_specs=[row] + specs,
        out_specs=row,
        out_shape=jax.ShapeDtypeStruct((t, d), F32),
        compiler_params=_params("parallel"),
        name="ffn",
    )(x, *ops)


def _mem_q(p, gain):
    heads = [_rms(p[:, h * MEM_DH:(h + 1) * MEM_DH], gain) * (MEM_DH ** -0.5) for h in range(MEM_HEADS)]
    return jnp.concatenate(heads, axis=-1).astype(BF16)


def _mem_head(qm, km_ref, vm_ref, h):
    hs = slice(h * MEM_DH, (h + 1) * MEM_DH)
    s = _dot_nt(qm[:, hs], km_ref[:, hs])
    p = jnp.exp(s - jnp.max(s, axis=-1, keepdims=True))
    l = jnp.sum(p, axis=-1, keepdims=True)
    return (_dot(p.astype(BF16), vm_ref[:, hs]) / l).astype(BF16)


def _gla_layer_body(x_ref, km_ref, vm_ref, g_ref, w_ref, wa_ref, ba_ref, mqg_ref, gn_ref, wo_ref,
                    out_ref, s_ref, o_s, *, dk, dv):
    c = GLA_CHUNK
    rows = x_ref.shape[0]
    sub = min(SUBTILE_ROWS, rows)
    dqk = GLA_HEADS * dk
    dvt = GLA_HEADS * dv
    dmem = MEM_HEADS * MEM_DH

    @pl.when(pl.program_id(1) == 0)
    def _():
        s_ref[...] = jnp.zeros_like(s_ref)

    ri = lax.broadcasted_iota(jnp.int32, (c, c), 0)
    ci = lax.broadcasted_iota(jnp.int32, (c, c), 1)
    causal = ci <= ri
    tril = jnp.where(causal, 1.0, 0.0).astype(BF16)
    gn = gn_ref[...]

    def projection(rs, res):
        def col(lo, n):
            return _dot(res["h"], w_ref[:, lo:lo + n])

        def s_a():
            res["h"] = _rms(x_ref[rs, :], g_ref[...]).astype(BF16)
            a_low = col(2 * dqk + 2 * dvt + dmem, LANES).astype(BF16)
            z = _dot(a_low, wa_ref[...]) + ba_ref[...]
            log_sig = jnp.minimum(z, 0.0) - jnp.log1p(jnp.exp(-jnp.abs(z)))
            res["la"] = log_sig * (1.0 / GLA_GATE_TAU)

        def s_q():
            res["q"] = col(0, dqk).astype(BF16)

        def s_cum():
            la = res["la"]
            hi = la.astype(BF16)
            rest = la - hi.astype(F32)
            mid = rest.astype(BF16)
            lo = (rest - mid.astype(F32)).astype(BF16)
            res["bc"] = [sum(_dot(tril, t[ic * c:(ic + 1) * c, :]) for t in (hi, mid, lo))
                         for ic in range(sub // c)]

        def s_k():
            res["k"] = col(dqk, dqk).astype(BF16)

        def s_v():
            res["v"] = col(2 * dqk, dvt).astype(BF16)

        def s_r():
            res["r"] = col(2 * dqk + dvt, dvt).astype(BF16)

        def s_m():
            res["qm"] = _mem_q(col(2 * dqk + 2 * dvt, dmem), mqg_ref[...])

        return [s_a, s_q, s_cum, s_k, s_v, s_r, s_m]

    def scan(r0, res):
        items = [(ic, h) for ic in range(sub // c) for h in range(GLA_HEADS)]
        ahead = []

        def independent_part(item):
            ic, h = item
            cs = slice(ic * c, (ic + 1) * c)
            ks = slice(h * dk, (h + 1) * dk)
            bc = res["bc"][ic][:, ks]
            b_last = bc[c - 1:c, :]
            b_mid = bc[c // 2 - 1:c // 2, :]
            q = res["q"][cs, ks].astype(F32) * (dk ** -0.5)
            k = res["k"][cs, ks].astype(F32)
            v = res["v"][cs, h * dv:(h + 1) * dv]
            qe = (q * jnp.exp(bc - b_mid)).astype(BF16)
            ke = (k * jnp.exp(b_mid - bc)).astype(BF16)
            sc = _dot_nt(qe, ke)
            u = _dot_tn((k * jnp.exp(b_last - bc)).astype(BF16), v)
            qi = (q * jnp.exp(bc)).astype(BF16)
            dcol = jnp.exp(jnp.broadcast_to(b_last, (dk, dk))).T
            ahead.append((sc, u, qi, v, dcol))

        def dependent_part(item):
            ic, h = item
            sc, u, qi, v, dcol = ahead.pop(0)
            cs = slice(ic * c, (ic + 1) * c)
            vs = slice(h * dv, (h + 1) * dv)
            s_prev = s_ref[h]
            o = _dot(jnp.where(causal, sc, 0.0).astype(BF16), v) + _dot(qi, s_prev.astype(BF16))
            s_ref[h] = jnp.concatenate([dcol] * (dv // dk), axis=1) * s_prev + u
            r = res["r"][cs, vs].astype(F32)
            o_s[r0 + ic * c:r0 + (ic + 1) * c, vs] = (_rms(o, gn) * (r * jax.nn.sigmoid(r))).astype(BF16)

        steps = [functools.partial(independent_part, it) for it in items[:GLA_LOOKAHEAD]]
        for i, it in enumerate(items):
            if i + GLA_LOOKAHEAD < len(items):
                steps.append(functools.partial(independent_part, items[i + GLA_LOOKAHEAD]))
            steps.append(functools.partial(dependent_part, it))
        return steps

    def output(rs, res):
        heads = []
        acc = {}

        def mem_head(h):
            heads.append(_mem_head(res["qm"], km_ref, vm_ref, h))

        def mix():
            acc["y"] = _dot(o_s[rs, :], wo_ref[:dvt, :])

        def finish():
            y = acc["y"] + _dot(jnp.concatenate(heads, axis=-1), wo_ref[dvt:, :])
            out_ref[rs, :] = x_ref[rs, :] + y

        return [functools.partial(mem_head, h) for h in range(MEM_HEADS)] + [mix, finish]

    slices = [slice(s * sub, (s + 1) * sub) for s in range(rows // sub)]
    results = [{} for _ in slices]
    _interleave(projection(slices[0], results[0]), [])
    for s in range(len(slices)):
        majors = []
        if s + 1 < len(slices):
            majors += projection(slices[s + 1], results[s + 1])
        if s >= 1:
            majors += output(slices[s - 1], results[s - 1])
        _interleave(majors, scan(s * sub, results[s]))
    _interleave(output(slices[-1], results[-1]), [])


def _gla_layer(x, k_mem, v_mem, params, batch, seq, dk, dv):
    t, d = x.shape
    tm = min(TOKEN_TILE, seq)
    nb = seq // tm
    row = pl.BlockSpec((tm, d), lambda b, i: (b * nb + i, 0))
    mem_blk = pl.BlockSpec((None,) + k_mem.shape[1:], lambda b, i: (b, 0, 0))
    specs, ops = _layered(params)
    return pl.pallas_call(
        functools.partial(_gla_layer_body, dk=dk, dv=dv),
        grid=(batch, nb),
        in_specs=[row, mem_blk, mem_blk] + specs,
        out_specs=row,
        out_shape=jax.ShapeDtypeStruct((t, d), F32),
        scratch_shapes=[pltpu.VMEM((GLA_HEADS, dk, dv), F32), pltpu.VMEM((tm, GLA_HEADS * dv), BF16)],
        compiler_params=_params("parallel", "arbitrary"),
        name="gla_layer",
    )(x, k_mem, v_mem, *ops)


def _rope(x, cos, sin):
    return x * cos + pltpu.roll(x, LANES // 2, axis=1) * sin


def _mla_in_body(x_ref, g_ref, w_ref, qn_ref, kvn_ref, wuq_ref, wuk_ref, wuvt_ref, qg_ref, kg_ref, krg_ref, mqg_ref,
                 cos_ref, sin_ref, q_out, k_out, vt_out, qm_out):
    o1 = MLA_Q_RANK
    o2 = o1 + MLA_KV_RANK
    qg = qg_ref[...]

    def matmuls(rs):
        h = _rms(x_ref[rs, :], g_ref[...]).astype(BF16)
        p = _dot(h, w_ref[...])
        cq = _rms(p[:, :o1], qn_ref[...]).astype(BF16)
        ckv = _rms(p[:, o1:o2], kvn_ref[...]).astype(BF16)
        return p[:, o2:], _dot(cq, wuq_ref[...]), _dot(ckv, wuk_ref[...]), _dot_nt(wuvt_ref[...], ckv)

    def epilogue(rs, p_tail, q, kn_all, vt):
        vt_out[:, rs] = vt.astype(BF16)
        qm_out[rs, :] = _mem_q(p_tail[:, LANES:], mqg_ref[...])
        cos = cos_ref[rs, :]
        sin = sin_ref[rs, :]
        k_rope = _rope(_rms(p_tail[:, :LANES], krg_ref[...], denom=MLA_ROPE), cos, sin).astype(BF16)
        for hd in range(MLA_HEADS):
            b0 = hd * MLA_QK_PAD
            qn = _rms(q[:, b0:b0 + LANES], qg[:, :LANES])
            qr = _rope(_rms(q[:, b0 + LANES:b0 + MLA_QK_PAD], qg[:, LANES:], denom=MLA_ROPE), cos, sin)
            q_out[rs, b0:b0 + LANES] = qn.astype(BF16)
            q_out[rs, b0 + LANES:b0 + MLA_QK_PAD] = qr.astype(BF16)
            kn = _rms(kn_all[:, hd * MLA_NOPE:(hd + 1) * MLA_NOPE], kg_ref[...])
            k_out[rs, b0:b0 + LANES] = kn.astype(BF16)
            k_out[rs, b0 + LANES:b0 + MLA_QK_PAD] = k_rope

    _pipelined_subtiles(x_ref.shape[0], matmuls, epilogue)


def _mla_in(x, params, cos, sin, seq):
    t, d = x.shape
    tm = min(TOKEN_TILE, t, seq)
    nb = seq // tm
    row = lambda n: pl.BlockSpec((tm, n), lambda i: (i, 0))
    pos = pl.BlockSpec((tm, LANES), lambda i: (i % nb, 0))
    dq = MLA_HEADS * MLA_QK_PAD
    dvt = MLA_HEADS * MLA_DV
    dmem = MEM_HEADS * MEM_DH
    specs, ops = _layered(params)
    return pl.pallas_call(
        _mla_in_body,
        grid=(t // tm,),
        in_specs=[row(d)] + specs + [pos, pos],
        out_specs=[row(dq), row(dq), pl.BlockSpec((dvt, tm), lambda i: (0, i)), row(dmem)],
        out_shape=[jax.ShapeDtypeStruct((t, dq), BF16), jax.ShapeDtypeStruct((t, dq), BF16),
                   jax.ShapeDtypeStruct((dvt, t), BF16), jax.ShapeDtypeStruct((t, dmem), BF16)],
        compiler_params=_params("parallel"),
        name="mla_in",
    )(x, *ops, cos, sin)


def _flash_body(q_ref, k_ref, vt_ref, o_ref, *, tq, tk, rc):
    qi = pl.program_id(2)
    n_diag = tq // tk
    n_chain = tq // rc
    dv = vt_ref.shape[0]
    ones = jnp.ones((ONES_ROWS, tk), BF16)

    def run_blocks(j0, state, diagonal):
        items = []
        for d in range(n_diag):
            for c in range(n_chain):
                lo_q, hi_q = c * rc, (c + 1) * rc - 1
                lo_k, hi_k = d * tk, (d + 1) * tk - 1
                if diagonal and lo_k > hi_q:
                    continue
                masked = diagonal and hi_k > lo_q
                assert not masked or lo_k <= lo_q
                items.append((d, c, masked))
        blocks = {}

        def key_block(d):
            if d not in blocks:
                k0 = pl.multiple_of((j0 + d) * tk, tk)
                blocks[d] = (k_ref[pl.ds(k0, tk), :],
                             jnp.concatenate([vt_ref[:, pl.ds(k0, tk)], ones], axis=0))
            return blocks[d]

        def scores(item):
            d, c, _ = item
            return _dot_nt(key_block(d)[0], q_ref[c * rc:(c + 1) * rc, :])

        state = list(state)
        ahead = [scores(it) for it in items[:FLASH_LOOKAHEAD]]
        for i, (d, c, masked) in enumerate(items):
            if i + FLASH_LOOKAHEAD < len(items):
                ahead.append(scores(items[i + FLASH_LOOKAHEAD]))
            st = ahead.pop(0)
            if masked:
                kpos = d * tk + lax.broadcasted_iota(jnp.int32, (tk, rc), 0)
                qpos = c * rc + lax.broadcasted_iota(jnp.int32, (tk, rc), 1)
                st = jnp.where(kpos <= qpos, st, NEG_BIG)
            m, acc = state[c]
            m_new = jnp.maximum(m, jnp.max(st, axis=0, keepdims=True))
            alpha = jnp.exp2(m - m_new)
            pt = jnp.exp2(st - m_new).astype(BF16)
            state[c] = (m_new, alpha * acc + _dot(key_block(d)[1], pt))
        return tuple(state)

    init = tuple((jnp.full((1, rc), NEG_BIG, F32), jnp.zeros((dv + ONES_ROWS, rc), F32)) for _ in range(n_chain))
    state = lax.fori_loop(0, qi, lambda t, st: run_blocks(t * n_diag, st, False), init)
    state = run_blocks(qi * n_diag, state, True)
    for c in range(n_chain):
        acc = state[c][1]
        o_ref[c * rc:(c + 1) * rc, :] = (acc[:dv, :] / acc[dv:dv + 1, :]).T.astype(o_ref.dtype)


def _flash(q, k, vt, batch, seq):
    tq = min(FLASH_TQ, seq)
    tk = min(FLASH_TK, tq)
    rc = min(FLASH_CHAIN_ROWS, tq)
    q3 = q.reshape(batch, seq, -1)
    k3 = k.reshape(batch, seq, -1)
    out = pl.pallas_call(
        functools.partial(_flash_body, tq=tq, tk=tk, rc=rc),
        grid=(batch, MLA_HEADS, seq // tq),
        in_specs=[pl.BlockSpec((None, tq, MLA_QK_PAD), lambda b, h, i: (b, i, h)),
                  pl.BlockSpec((None, seq, MLA_QK_PAD), lambda b, h, i: (b, 0, h)),
                  pl.BlockSpec((MLA_DV, seq), lambda b, h, i: (h, b))],
        out_specs=pl.BlockSpec((None, tq, MLA_DV), lambda b, h, i: (b, i, h)),
        out_shape=jax.ShapeDtypeStruct((batch, seq, MLA_HEADS * MLA_DV), BF16),
        compiler_params=_params("parallel", "parallel", "arbitrary"),
        name="mla_flash",
    )(q3, k3, vt)
    return out.reshape(batch * seq, -1)


def _memkv_body(mem_ref, g_ref, w_ref, kg_ref, k_out, v_out):
    m = _rms(mem_ref[...], g_ref[...]).astype(BF16)
    kv = _dot(m, w_ref[...])
    dm = MEM_HEADS * MEM_DH
    heads = [_rms(kv[:, h * MEM_DH:(h + 1) * MEM_DH], kg_ref[...]) for h in range(MEM_HEADS)]
    k_out[...] = jnp.concatenate(heads, axis=-1).astype(BF16)
    v_out[...] = kv[:, dm:].astype(BF16)


def _memkv(mem, params):
    b, m, d = mem.shape
    specs, ops = _layered(params)
    dm = MEM_HEADS * MEM_DH
    blk = pl.BlockSpec((None, m, dm), lambda i: (i, 0, 0))
    return pl.pallas_call(
        _memkv_body,
        grid=(b,),
        in_specs=[pl.BlockSpec((None, m, d), lambda i: (i, 0, 0))] + specs,
        out_specs=[blk, blk],
        out_shape=[jax.ShapeDtypeStruct((b, m, dm), BF16)] * 2,
        compiler_params=_params("parallel"),
        name="mem_kv",
    )(mem, *ops)


def _out_body(x_ref, o_ref, qm_ref, km_ref, vm_ref, wo_ref, out_ref):
    qm = qm_ref[...]
    om = jnp.concatenate([_mem_head(qm, km_ref, vm_ref, h) for h in range(MEM_HEADS)], axis=-1)
    dmix = o_ref.shape[-1]
    y = _dot(o_ref[...], wo_ref[:dmix, :]) + _dot(om, wo_ref[dmix:, :])
    out_ref[...] = x_ref[...] + y


def _out_proj(x, o, q_mem, k_mem, v_mem, w_out, seq):
    t, d = x.shape
    tm = min(TOKEN_TILE, t, seq)
    nb = seq // tm
    row = lambda n: pl.BlockSpec((tm, n), lambda i: (i, 0))
    mem_blk = pl.BlockSpec((None,) + k_mem.shape[1:], lambda i: (i // nb, 0, 0))
    return pl.pallas_call(
        _out_body,
        grid=(t // tm,),
        in_specs=[row(d), row(o.shape[1]), row(q_mem.shape[1]), mem_blk, mem_blk, _layer_spec(*w_out)],
        out_specs=row(d),
        out_shape=jax.ShapeDtypeStruct((t, d), F32),
        compiler_params=_params("parallel"),
        name="out_proj",
    )(x, o, q_mem, k_mem, v_mem, w_out[0])


def _pad_rope(w):
    half = MLA_ROPE // 2
    z = jnp.zeros(w.shape[:-1] + (LANES // 2 - half,), w.dtype)
    return jnp.concatenate([w[..., :half], z, w[..., half:], z], axis=-1)


def _rope_tables(seq):
    inv = 1.0 / (ROPE_BASE ** (jnp.arange(0, MLA_ROPE, 2, dtype=F32) / MLA_ROPE))
    ang = jnp.arange(seq, dtype=F32)[:, None] * inv[None, :]
    cos = _pad_rope(jnp.concatenate([jnp.cos(ang)] * 2, axis=-1))
    sin = _pad_rope(jnp.concatenate([-jnp.sin(ang), jnp.sin(ang)], axis=-1))
    return cos, sin


def kernel(x, mem, ffn1_norm, ffn1_w_gate, ffn1_w_up, ffn1_w_down, ffn2_norm, ffn2_w_gate, ffn2_w_up, ffn2_w_down, mix_norm, w_out, mem_norm, mem_w_kv, memq_norm, memk_norm, gla_w_in, gla_w_alpha, gla_b_alpha, gla_out_norm, mla_w_in, mla_q_norm, mla_kv_norm, mla_w_uq, mla_w_ukv, mla_qn_norm, mla_qr_norm, mla_kn_norm, mla_kr_norm):
    batch, seq, d = x.shape
    depth = ffn1_norm.shape[0]
    t = batch * seq
    xs = x.reshape(t, d)
    cos, sin = _rope_tables(seq)
    gains = lambda a: a.astype(F32)[:, None, :]
    bf = lambda a: a.astype(BF16)

    ffn1 = [gains(ffn1_norm), bf(ffn1_w_gate), bf(ffn1_w_up), bf(ffn1_w_down)]
    ffn2 = [gains(ffn2_norm), bf(ffn2_w_gate), bf(ffn2_w_up), bf(ffn2_w_down)]
    mix_g, memq_g = gains(mix_norm), gains(memq_norm)
    memkv = [gains(mem_norm), bf(mem_w_kv), gains(memk_norm)]
    w_out_b = bf(w_out)

    gla_dqk = gla_w_alpha.shape[-1]
    gla_dv = d
    a_lo = 2 * gla_dqk + gla_dv
    a_hi = a_lo + GLA_GATE_RANK
    pad_rank = LANES - GLA_GATE_RANK
    gla_w = bf(jnp.concatenate([gla_w_in[..., :a_lo], gla_w_in[..., a_hi:],
                                jnp.pad(gla_w_in[..., a_lo:a_hi], ((0, 0), (0, 0), (0, pad_rank)))], axis=-1))
    gla_wa = bf(jnp.pad(gla_w_alpha, ((0, 0), (0, pad_rank), (0, 0))))
    gla_ba, gla_og = gains(gla_b_alpha), gains(gla_out_norm)

    r_lo = MLA_Q_RANK + MLA_KV_RANK
    r_hi = r_lo + MLA_ROPE
    n_mla = mla_w_in.shape[0]
    mla_w = bf(jnp.concatenate([mla_w_in[..., :r_lo], _pad_rope(mla_w_in[..., r_lo:r_hi]),
                                mla_w_in[..., r_hi:]], axis=-1))
    uq = mla_w_uq.reshape(n_mla, MLA_Q_RANK, MLA_HEADS, MLA_NOPE + MLA_ROPE)
    w_uq = bf(jnp.concatenate([uq[..., :MLA_NOPE], _pad_rope(uq[..., MLA_NOPE:])], axis=-1)
              .reshape(n_mla, MLA_Q_RANK, MLA_HEADS * MLA_QK_PAD))
    ukv = mla_w_ukv.reshape(n_mla, MLA_KV_RANK, MLA_HEADS, MLA_NOPE + MLA_DV)
    w_uk = bf(ukv[..., :MLA_NOPE].reshape(n_mla, MLA_KV_RANK, -1))
    w_uvt = bf(jnp.swapaxes(ukv[..., MLA_NOPE:].reshape(n_mla, MLA_KV_RANK, -1), 1, 2))
    qk_scale = (MLA_NOPE + MLA_ROPE) ** -0.5 * LOG2_E
    mla_qg = gains(jnp.concatenate([mla_qn_norm, _pad_rope(mla_qr_norm)], axis=-1) * qk_scale)
    mla_qn, mla_kvn = gains(mla_q_norm), gains(mla_kv_norm)
    mla_kg, mla_krg = gains(mla_kn_norm), gains(_pad_rope(mla_kr_norm))

    for i in range(depth):
        j = i // 2
        xs = _ffn(xs, [(a, i) for a in ffn1])
        k_mem, v_mem = _memkv(mem, [(a, i) for a in memkv])
        if i % 2 == 0:
            xs = _gla_layer(xs, k_mem, v_mem, [(mix_g, i), (gla_w, j), (gla_wa, j), (gla_ba, j), (memq_g, i),
                                               (gla_og, j), (w_out_b, i)],
                            batch, seq, gla_dqk // GLA_HEADS, gla_dv // GLA_HEADS)
        else:
            q, k, vt, q_mem = _mla_in(xs, [(mix_g, i), (mla_w, j), (mla_qn, j), (mla_kvn, j), (w_uq, j), (w_uk, j),
                                           (w_uvt, j), (mla_qg, j), (mla_kg, j), (mla_krg, j), (memq_g, i)],
                                      cos, sin, seq)
            o = _flash(q, k, vt, batch, seq)
            xs = _out_proj(xs, o, q_mem, k_mem, v_mem, (w_out_b, i), seq)
        xs = _ffn(xs, [(a, i) for a in ffn2])
    return xs.reshape(batch, seq, d)
```

```python
import functools

import jax
import jax.numpy as jnp
from jax import lax
from jax.experimental import pallas as pl
from jax.experimental.pallas import tpu as pltpu

F32 = jnp.float32
BF16 = jnp.bfloat16

EPS = 1e-6
GLA_HEADS = 4
GLA_GATE_RANK = 16
GLA_GATE_TAU = 16.0
GLA_CHUNK = 64
MLA_HEADS = 8
MLA_Q_RANK = 384
MLA_KV_RANK = 256
MLA_NOPE = 128
MLA_ROPE = 64
MLA_DV = 128
ROPE_BASE = 10000.0
MEM_HEADS = 4
MEM_DH = 128

LANES = 128
MLA_QK_PAD = 2 * LANES
VMEM_LIMIT_BYTES = 56 * 1024 * 1024

TOKEN_TILE = 512
SUBTILE_ROWS = 256
FLASH_TQ = 2048
FLASH_TK = 256
FLASH_CHAIN_ROWS = 256
FLASH_LOOKAHEAD = 16
ONES_ROWS = 16
GLA_LOOKAHEAD = 4

NEG_BIG = -1e30
LOG2_E = 1.4426950408889634


def _params(*sem):
    return pltpu.CompilerParams(dimension_semantics=sem, vmem_limit_bytes=VMEM_LIMIT_BYTES)


def _layer_spec(stacked, layer):
    nd = stacked.ndim - 1
    return pl.BlockSpec((None,) + stacked.shape[1:], lambda *_: (layer,) + (0,) * nd,
                        pipeline_mode=pl.Buffered(1))


def _layered(pairs):
    return [_layer_spec(a, l) for a, l in pairs], [a for a, _ in pairs]


def _rms(x, gain, denom=None):
    d = x.shape[-1] if denom is None else denom
    ms = jnp.sum(x * x, axis=-1, keepdims=True) * (1.0 / d)
    return x * lax.rsqrt(ms + EPS) * gain


def _dot(a, b):
    return jnp.dot(a, b, preferred_element_type=F32)


def _dot_nt(a, b):
    return lax.dot_general(a, b, (((1,), (1,)), ((), ())), preferred_element_type=F32)


def _dot_tn(a, b):
    return lax.dot_general(a, b, (((0,), (0,)), ((), ())), preferred_element_type=F32)


def _pipelined_subtiles(rows, matmuls, epilogue):
    sub = min(SUBTILE_ROWS, rows)
    slices = [slice(s * sub, (s + 1) * sub) for s in range(rows // sub)]
    staged = matmuls(slices[0])
    for s, rs in enumerate(slices):
        ready = staged
        if s + 1 < len(slices):
            staged = matmuls(slices[s + 1])
        epilogue(rs, *ready)


def _interleave(major, minor):
    per = -(-len(minor) // max(len(major), 1))
    for i, step in enumerate(major):
        step()
        for sub in minor[i * per:(i + 1) * per]:
            sub()
    for sub in minor[len(major) * per:]:
        sub()


def _ffn_body(x_ref, g_ref, wg_ref, wu_ref, wd_ref, o_ref):
    def matmuls(rs):
        h = _rms(x_ref[rs, :], g_ref[...]).astype(BF16)
        return _dot(h, wg_ref[...]), _dot(h, wu_ref[...])

    def epilogue(rs, gate, up):
        a = (gate * jax.nn.sigmoid(gate) * up).astype(BF16)
        o_ref[rs, :] = x_ref[rs, :] + 0.5 * _dot(a, wd_ref[...])

    _pipelined_subtiles(x_ref.shape[0], matmuls, epilogue)


def _ffn(x, params):
    t, d = x.shape
    tm = min(TOKEN_TILE, t)
    row = pl.BlockSpec((tm, d), lambda i: (i, 0))
    specs, ops = _layered(params)
    return pl.pallas_call(
        _ffn_body,
        grid=(t // tm,),
        in_specs=[row] + specs,
        out_specs=row,
        out_shape=jax.ShapeDtypeStruct((t, d), F32),
        compiler_params=_params("parallel"),
        name="ffn",
    )(x, *ops)


def _mem_q(p, gain):
    heads = [_rms(p[:, h * MEM_DH:(h + 1) * MEM_DH], gain) * (MEM_DH ** -0.5) for h in range(MEM_HEADS)]
    return jnp.concatenate(heads, axis=-1).astype(BF16)


def _mem_head(qm, km_ref, vm_ref, h):
    hs = slice(h * MEM_DH, (h + 1) * MEM_DH)
    s = _dot_nt(qm[:, hs], km_ref[:, hs])
    p = jnp.exp(s - jnp.max(s, axis=-1, keepdims=True))
    l = jnp.sum(p, axis=-1, keepdims=True)
    return (_dot(p.astype(BF16), vm_ref[:, hs]) / l).astype(BF16)


def _gla_layer_body(x_ref, km_ref, vm_ref, g_ref, w_ref, wa_ref, ba_ref, mqg_ref, gn_ref, wo_ref,
                    out_ref, s_ref, o_s, *, dk, dv):
    c = GLA_CHUNK
    rows = x_ref.shape[0]
    sub = min(SUBTILE_ROWS, rows)
    dqk = GLA_HEADS * dk
    dvt = GLA_HEADS * dv
    dmem = MEM_HEADS * MEM_DH

    @pl.when(pl.program_id(1) == 0)
    def _():
        s_ref[...] = jnp.zeros_like(s_ref)

    ri = lax.broadcasted_iota(jnp.int32, (c, c), 0)
    ci = lax.broadcasted_iota(jnp.int32, (c, c), 1)
    causal = ci <= ri
    tril = jnp.where(causal, 1.0, 0.0).astype(BF16)
    gn = gn_ref[...]

    def projection(rs, res):
        def col(lo, n):
            return _dot(res["h"], w_ref[:, lo:lo + n])

        def s_a():
            res["h"] = _rms(x_ref[rs, :], g_ref[...]).astype(BF16)
            a_low = col(2 * dqk + 2 * dvt + dmem, LANES).astype(BF16)
            z = _dot(a_low, wa_ref[...]) + ba_ref[...]
            log_sig = jnp.minimum(z, 0.0) - jnp.log1p(jnp.exp(-jnp.abs(z)))
            res["la"] = log_sig * (1.0 / GLA_GATE_TAU)

        def s_q():
            res["q"] = col(0, dqk).astype(BF16)

        def s_cum():
            la = res["la"]
            hi = la.astype(BF16)
            rest = la - hi.astype(F32)
            mid = rest.astype(BF16)
            lo = (rest - mid.astype(F32)).astype(BF16)
            res["bc"] = [sum(_dot(tril, t[ic * c:(ic + 1) * c, :]) for t in (hi, mid, lo))
                         for ic in range(sub // c)]

        def s_k():
            res["k"] = col(dqk, dqk).astype(BF16)

        def s_v():
            res["v"] = col(2 * dqk, dvt).astype(BF16)

        def s_r():
            res["r"] = col(2 * dqk + dvt, dvt).astype(BF16)

        def s_m():
            res["qm"] = _mem_q(col(2 * dqk + 2 * dvt, dmem), mqg_ref[...])

        return [s_a, s_q, s_cum, s_k, s_v, s_r, s_m]

    def scan(r0, res):
        items = [(ic, h) for ic in range(sub // c) for h in range(GLA_HEADS)]
        ahead = []

        def independent_part(item):
            ic, h = item
            cs = slice(ic * c, (ic + 1) * c)
            ks = slice(h * dk, (h + 1) * dk)
            bc = res["bc"][ic][:, ks]
            b_last = bc[c - 1:c, :]
            b_mid = bc[c // 2 - 1:c // 2, :]
            q = res["q"][cs, ks].astype(F32) * (dk ** -0.5)
            k = res["k"][cs, ks].astype(F32)
            v = res["v"][cs, h * dv:(h + 1) * dv]
            qe = (q * jnp.exp(bc - b_mid)).astype(BF16)
            ke = (k * jnp.exp(b_mid - bc)).astype(BF16)
            sc = _dot_nt(qe, ke)
            u = _dot_tn((k * jnp.exp(b_last - bc)).astype(BF16), v)
            qi = (q * jnp.exp(bc)).astype(BF16)
            dcol = jnp.exp(jnp.broadcast_to(b_last, (dk, dk))).T
            ahead.append((sc, u, qi, v, dcol))

        def dependent_part(item):
            ic, h = item
            sc, u, qi, v, dcol = ahead.pop(0)
            cs = slice(ic * c, (ic + 1) * c)
            vs = slice(h * dv, (h + 1) * dv)
            s_prev = s_ref[h]
            o = _dot(jnp.where(causal, sc, 0.0).astype(BF16), v) + _dot(qi, s_prev.astype(BF16))
            s_ref[h] = jnp.concatenate([dcol] * (dv // dk), axis=1) * s_prev + u
            r = res["r"][cs, vs].astype(F32)
            o_s[r0 + ic * c:r0 + (ic + 1) * c, vs] = (_rms(o, gn) * (r * jax.nn.sigmoid(r))).astype(BF16)

        steps = [functools.partial(independent_part, it) for it in items[:GLA_LOOKAHEAD]]
        for i, it in enumerate(items):
            if i + GLA_LOOKAHEAD < len(items):
                steps.append(functools.partial(independent_part, items[i + GLA_LOOKAHEAD]))
            steps.append(functools.partial(dependent_part, it))
        return steps

    def output(rs, res):
        heads = []
        acc = {}

        def mem_head(h):
            heads.append(_mem_head(res["qm"], km_ref, vm_ref, h))

        def mix():
            acc["y"] = _dot(o_s[rs, :], wo_ref[:dvt, :])

        def finish():
            y = acc["y"] + _dot(jnp.concatenate(heads, axis=-1), wo_ref[dvt:, :])
            out_ref[rs, :] = x_ref[rs, :] + y

        return [functools.partial(mem_head, h) for h in range(MEM_HEADS)] + [mix, finish]

    slices = [slice(s * sub, (s + 1) * sub) for s in range(rows // sub)]
    results = [{} for _ in slices]
    _interleave(projection(slices[0], results[0]), [])
    for s in range(len(slices)):
        majors = []
        if s + 1 < len(slices):
            majors += projection(slices[s + 1], results[s + 1])
        if s >= 1:
            majors += output(slices[s - 1], results[s - 1])
        _interleave(majors, scan(s * sub, results[s]))
    _interleave(output(slices[-1], results[-1]), [])


def _gla_layer(x, k_mem, v_mem, params, batch, seq, dk, dv):
    t, d = x.shape
    tm = min(TOKEN_TILE, seq)
    nb = seq // tm
    row = pl.BlockSpec((tm, d), lambda b, i: (b * nb + i, 0))
    mem_blk = pl.BlockSpec((None,) + k_mem.shape[1:], lambda b, i: (b, 0, 0))
    specs, ops = _layered(params)
    return pl.pallas_call(
        functools.partial(_gla_layer_body, dk=dk, dv=dv),
        grid=(batch, nb),
        in_specs=[row, mem_blk, mem_blk] + specs,
        out_specs=row,
        out_shape=jax.ShapeDtypeStruct((t, d), F32),
        scratch_shapes=[pltpu.VMEM((GLA_HEADS, dk, dv), F32), pltpu.VMEM((tm, GLA_HEADS * dv), BF16)],
        compiler_params=_params("parallel", "arbitrary"),
        name="gla_layer",
    )(x, k_mem, v_mem, *ops)


def _rope(x, cos, sin):
    return x * cos + pltpu.roll(x, LANES // 2, axis=1) * sin


def _ffn_mla_in_body(x_ref, fg_ref, wg_ref, wu_ref, wd_ref,
                     g_ref, w_ref, qn_ref, kvn_ref, wuq_ref, wuk_ref, wuvt_ref, qg_ref, kg_ref, krg_ref, mqg_ref,
                     cos_ref, sin_ref, x_out, q_out, k_out, vt_out, qm_out):
    o1 = MLA_Q_RANK
    o2 = o1 + MLA_KV_RANK
    qg = qg_ref[...]

    def ffn_in(rs):
        h = _rms(x_ref[rs, :], fg_ref[...]).astype(BF16)
        return _dot(h, wg_ref[...]), _dot(h, wu_ref[...])

    def ffn_out(rs, gate, up):
        a = (gate * jax.nn.sigmoid(gate) * up).astype(BF16)
        xn = x_ref[rs, :] + 0.5 * _dot(a, wd_ref[...])
        x_out[rs, :] = xn
        return xn

    def matmuls(xn):
        h = _rms(xn, g_ref[...]).astype(BF16)
        p = _dot(h, w_ref[...])
        cq = _rms(p[:, :o1], qn_ref[...]).astype(BF16)
        ckv = _rms(p[:, o1:o2], kvn_ref[...]).astype(BF16)
        return p[:, o2:], _dot(cq, wuq_ref[...]), _dot(ckv, wuk_ref[...]), _dot_nt(wuvt_ref[...], ckv)

    def epilogue(rs, p_tail, q, kn_all, vt):
        vt_out[:, rs] = vt.astype(BF16)
        qm_out[rs, :] = _mem_q(p_tail[:, LANES:], mqg_ref[...])
        cos = cos_ref[rs, :]
        sin = sin_ref[rs, :]
        k_rope = _rope(_rms(p_tail[:, :LANES], krg_ref[...], denom=MLA_ROPE), cos, sin).astype(BF16)
        for hd in range(MLA_HEADS):
            b0 = hd * MLA_QK_PAD
            qn = _rms(q[:, b0:b0 + LANES], qg[:, :LANES])
            qr = _rope(_rms(q[:, b0 + LANES:b0 + MLA_QK_PAD], qg[:, LANES:], denom=MLA_ROPE), cos, sin)
            q_out[rs, b0:b0 + LANES] = qn.astype(BF16)
            q_out[rs, b0 + LANES:b0 + MLA_QK_PAD] = qr.astype(BF16)
            kn = _rms(kn_all[:, hd * MLA_NOPE:(hd + 1) * MLA_NOPE], kg_ref[...])
            k_out[rs, b0:b0 + LANES] = kn.astype(BF16)
            k_out[rs, b0 + LANES:b0 + MLA_QK_PAD] = k_rope

    rows = x_ref.shape[0]
    sub = min(SUBTILE_ROWS, rows)
    slices = [slice(s * sub, (s + 1) * sub) for s in range(rows // sub)]
    hidden = [ffn_in(rs) for rs in slices]
    staged = []
    for rs, gate_up in zip(slices, hidden):
        staged.append(matmuls(ffn_out(rs, *gate_up)))
    for rs, ready in zip(slices, staged):
        epilogue(rs, *ready)


def _ffn_mla_in(x, params, cos, sin, seq):
    t, d = x.shape
    tm = min(TOKEN_TILE, t, seq)
    nb = seq // tm
    row = lambda n: pl.BlockSpec((tm, n), lambda i: (i, 0))
    pos = pl.BlockSpec((tm, LANES), lambda i: (i % nb, 0))
    dq = MLA_HEADS * MLA_QK_PAD
    dvt = MLA_HEADS * MLA_DV
    dmem = MEM_HEADS * MEM_DH
    specs, ops = _layered(params)
    return pl.pallas_call(
        _ffn_mla_in_body,
        grid=(t // tm,),
        in_specs=[row(d)] + specs + [pos, pos],
        out_specs=[row(d), row(dq), row(dq), pl.BlockSpec((dvt, tm), lambda i: (0, i)), row(dmem)],
        out_shape=[jax.ShapeDtypeStruct((t, d), F32), jax.ShapeDtypeStruct((t, dq), BF16),
                   jax.ShapeDtypeStruct((t, dq), BF16), jax.ShapeDtypeStruct((dvt, t), BF16),
                   jax.ShapeDtypeStruct((t, dmem), BF16)],
        compiler_params=_params("parallel"),
        name="ffn_mla_in",
    )(x, *ops, cos, sin)


def _flash_body(q_ref, k_ref, vt_ref, o_ref, *, tq, tk, rc):
    qi = pl.program_id(2)
    n_diag = tq // tk
    n_chain = tq // rc
    dv = vt_ref.shape[0]
    ones = jnp.ones((ONES_ROWS, tk), BF16)

    def run_blocks(j0, state, diagonal):
        items = []
        for d in range(n_diag):
            for c in range(n_chain):
                lo_q, hi_q = c * rc, (c + 1) * rc - 1
                lo_k, hi_k = d * tk, (d + 1) * tk - 1
                if diagonal and lo_k > hi_q:
                    continue
                masked = diagonal and hi_k > lo_q
                assert not masked or lo_k <= lo_q
                items.append((d, c, masked))
        blocks = {}

        def key_block(d):
            if d not in blocks:
                k0 = pl.multiple_of((j0 + d) * tk, tk)
                blocks[d] = (k_ref[pl.ds(k0, tk), :],
                             jnp.concatenate([vt_ref[:, pl.ds(k0, tk)], ones], axis=0))
            return blocks[d]

        def scores(item):
            d, c, _ = item
            return _dot_nt(key_block(d)[0], q_ref[c * rc:(c + 1) * rc, :])

        state = list(state)
        ahead = [scores(it) for it in items[:FLASH_LOOKAHEAD]]
        for i, (d, c, masked) in enumerate(items):
            if i + FLASH_LOOKAHEAD < len(items):
                ahead.append(scores(items[i + FLASH_LOOKAHEAD]))
            st = ahead.pop(0)
            if masked:
                kpos = d * tk + lax.broadcasted_iota(jnp.int32, (tk, rc), 0)
                qpos = c * rc + lax.broadcasted_iota(jnp.int32, (tk, rc), 1)
                st = jnp.where(kpos <= qpos, st, NEG_BIG)
            m, acc = state[c]
            m_new = jnp.maximum(m, jnp.max(st, axis=0, keepdims=True))
            alpha = jnp.exp2(m - m_new)
            pt = jnp.exp2(st - m_new).astype(BF16)
            state[c] = (m_new, alpha * acc + _dot(key_block(d)[1], pt))
        return tuple(state)

    init = tuple((jnp.full((1, rc), NEG_BIG, F32), jnp.zeros((dv + ONES_ROWS, rc), F32)) for _ in range(n_chain))
    state = lax.fori_loop(0, qi, lambda t, st: run_blocks(t * n_diag, st, False), init)
    state = run_blocks(qi * n_diag, state, True)
    for c in range(n_chain):
        acc = state[c][1]
        o_ref[c * rc:(c + 1) * rc, :] = (acc[:dv, :] / acc[dv:dv + 1, :]).T.astype(o_ref.dtype)


def _flash(q, k, vt, batch, seq):
    tq = min(FLASH_TQ, seq)
    tk = min(FLASH_TK, tq)
    rc = min(FLASH_CHAIN_ROWS, tq)
    q3 = q.reshape(batch, seq, -1)
    k3 = k.reshape(batch, seq, -1)
    out = pl.pallas_call(
        functools.partial(_flash_body, tq=tq, tk=tk, rc=rc),
        grid=(batch, MLA_HEADS, seq // tq),
        in_specs=[pl.BlockSpec((None, tq, MLA_QK_PAD), lambda b, h, i: (b, i, h)),
                  pl.BlockSpec((None, seq, MLA_QK_PAD), lambda b, h, i: (b, 0, h)),
                  pl.BlockSpec((MLA_DV, seq), lambda b, h, i: (h, b))],
        out_specs=pl.BlockSpec((None, tq, MLA_DV), lambda b, h, i: (b, i, h)),
        out_shape=jax.ShapeDtypeStruct((batch, seq, MLA_HEADS * MLA_DV), BF16),
        compiler_params=_params("parallel", "parallel", "arbitrary"),
        name="mla_flash",
    )(q3, k3, vt)
    return out.reshape(batch * seq, -1)


def _memkv_body(mem_ref, g_ref, w_ref, kg_ref, k_out, v_out):
    m = _rms(mem_ref[...], g_ref[...]).astype(BF16)
    kv = _dot(m, w_ref[...])
    dm = MEM_HEADS * MEM_DH
    heads = [_rms(kv[:, h * MEM_DH:(h + 1) * MEM_DH], kg_ref[...]) for h in range(MEM_HEADS)]
    k_out[...] = jnp.concatenate(heads, axis=-1).astype(BF16)
    v_out[...] = kv[:, dm:].astype(BF16)


def _memkv(mem, params):
    b, m, d = mem.shape
    specs, ops = _layered(params)
    dm = MEM_HEADS * MEM_DH
    blk = pl.BlockSpec((None, m, dm), lambda i: (i, 0, 0))
    return pl.pallas_call(
        _memkv_body,
        grid=(b,),
        in_specs=[pl.BlockSpec((None, m, d), lambda i: (i, 0, 0))] + specs,
        out_specs=[blk, blk],
        out_shape=[jax.ShapeDtypeStruct((b, m, dm), BF16)] * 2,
        compiler_params=_params("parallel"),
        name="mem_kv",
    )(mem, *ops)


def _out_body(x_ref, o_ref, qm_ref, km_ref, vm_ref, wo_ref, out_ref):
    qm = qm_ref[...]
    om = jnp.concatenate([_mem_head(qm, km_ref, vm_ref, h) for h in range(MEM_HEADS)], axis=-1)
    dmix = o_ref.shape[-1]
    y = _dot(o_ref[...], wo_ref[:dmix, :]) + _dot(om, wo_ref[dmix:, :])
    out_ref[...] = x_ref[...] + y


def _out_proj(x, o, q_mem, k_mem, v_mem, w_out, seq):
    t, d = x.shape
    tm = min(TOKEN_TILE, t, seq)
    nb = seq // tm
    row = lambda n: pl.BlockSpec((tm, n), lambda i: (i, 0))
    mem_blk = pl.BlockSpec((None,) + k_mem.shape[1:], lambda i: (i // nb, 0, 0))
    return pl.pallas_call(
        _out_body,
        grid=(t // tm,),
        in_specs=[row(d), row(o.shape[1]), row(q_mem.shape[1]), mem_blk, mem_blk, _layer_spec(*w_out)],
        out_specs=row(d),
        out_shape=jax.ShapeDtypeStruct((t, d), F32),
        compiler_params=_params("parallel"),
        name="out_proj",
    )(x, o, q_mem, k_mem, v_mem, w_out[0])


def _pad_rope(w):
    half = MLA_ROPE // 2
    z = jnp.zeros(w.shape[:-1] + (LANES // 2 - half,), w.dtype)
    return jnp.concatenate([w[..., :half], z, w[..., half:], z], axis=-1)


def _rope_tables(seq):
    inv = 1.0 / (ROPE_BASE ** (jnp.arange(0, MLA_ROPE, 2, dtype=F32) / MLA_ROPE))
    ang = jnp.arange(seq, dtype=F32)[:, None] * inv[None, :]
    cos = _pad_rope(jnp.concatenate([jnp.cos(ang)] * 2, axis=-1))
    sin = _pad_rope(jnp.concatenate([-jnp.sin(ang), jnp.sin(ang)], axis=-1))
    return cos, sin


def kernel(x, mem, ffn1_norm, ffn1_w_gate, ffn1_w_up, ffn1_w_down, ffn2_norm, ffn2_w_gate, ffn2_w_up, ffn2_w_down, mix_norm, w_out, mem_norm, mem_w_kv, memq_norm, memk_norm, gla_w_in, gla_w_alpha, gla_b_alpha, gla_out_norm, mla_w_in, mla_q_norm, mla_kv_norm, mla_w_uq, mla_w_ukv, mla_qn_norm, mla_qr_norm, mla_kn_norm, mla_kr_norm):
    batch, seq, d = x.shape
    depth = ffn1_norm.shape[0]
    t = batch * seq
    xs = x.reshape(t, d)
    cos, sin = _rope_tables(seq)
    gains = lambda a: a.astype(F32)[:, None, :]
    bf = lambda a: a.astype(BF16)

    ffn1 = [gains(ffn1_norm), bf(ffn1_w_gate), bf(ffn1_w_up), bf(ffn1_w_down)]
    ffn2 = [gains(ffn2_norm), bf(ffn2_w_gate), bf(ffn2_w_up), bf(ffn2_w_down)]
    mix_g, memq_g = gains(mix_norm), gains(memq_norm)
    memkv = [gains(mem_norm), bf(mem_w_kv), gains(memk_norm)]
    w_out_b = bf(w_out)

    gla_dqk = gla_w_alpha.shape[-1]
    gla_dv = d
    a_lo = 2 * gla_dqk + gla_dv
    a_hi = a_lo + GLA_GATE_RANK
    pad_rank = LANES - GLA_GATE_RANK
    gla_w = bf(jnp.concatenate([gla_w_in[..., :a_lo], gla_w_in[..., a_hi:],
                                jnp.pad(gla_w_in[..., a_lo:a_hi], ((0, 0), (0, 0), (0, pad_rank)))], axis=-1))
    gla_wa = bf(jnp.pad(gla_w_alpha, ((0, 0), (0, pad_rank), (0, 0))))
    gla_ba, gla_og = gains(gla_b_alpha), gains(gla_out_norm)

    r_lo = MLA_Q_RANK + MLA_KV_RANK
    r_hi = r_lo + MLA_ROPE
    n_mla = mla_w_in.shape[0]
    mla_w = bf(jnp.concatenate([mla_w_in[..., :r_lo], _pad_rope(mla_w_in[..., r_lo:r_hi]),
                                mla_w_in[..., r_hi:]], axis=-1))
    uq = mla_w_uq.reshape(n_mla, MLA_Q_RANK, MLA_HEADS, MLA_NOPE + MLA_ROPE)
    w_uq = bf(jnp.concatenate([uq[..., :MLA_NOPE], _pad_rope(uq[..., MLA_NOPE:])], axis=-1)
              .reshape(n_mla, MLA_Q_RANK, MLA_HEADS * MLA_QK_PAD))
    ukv = mla_w_ukv.reshape(n_mla, MLA_KV_RANK, MLA_HEADS, MLA_NOPE + MLA_DV)
    w_uk = bf(ukv[..., :MLA_NOPE].reshape(n_mla, MLA_KV_RANK, -1))
    w_uvt = bf(jnp.swapaxes(ukv[..., MLA_NOPE:].reshape(n_mla, MLA_KV_RANK, -1), 1, 2))
    qk_scale = (MLA_NOPE + MLA_ROPE) ** -0.5 * LOG2_E
    mla_qg = gains(jnp.concatenate([mla_qn_norm, _pad_rope(mla_qr_norm)], axis=-1) * qk_scale)
    mla_qn, mla_kvn = gains(mla_q_norm), gains(mla_kv_norm)
    mla_kg, mla_krg = gains(mla_kn_norm), gains(_pad_rope(mla_kr_norm))

    for i in range(depth):
        j = i // 2
        k_mem, v_mem = _memkv(mem, [(a, i) for a in memkv])
        if i % 2 == 0:
            xs = _ffn(xs, [(a, i) for a in ffn1])
            xs = _gla_layer(xs, k_mem, v_mem, [(mix_g, i), (gla_w, j), (gla_wa, j), (gla_ba, j), (memq_g, i),
                                               (gla_og, j), (w_out_b, i)],
                            batch, seq, gla_dqk // GLA_HEADS, gla_dv // GLA_HEADS)
        else:
            xs, q, k, vt, q_mem = _ffn_mla_in(
                xs, [(a, i) for a in ffn1] + [(mix_g, i), (mla_w, j), (mla_qn, j), (mla_kvn, j), (w_uq, j),
                                             (w_uk, j), (w_uvt, j), (mla_qg, j), (mla_kg, j), (mla_krg, j),
                                             (memq_g, i)], cos, sin, seq)
            o = _flash(q, k, vt, batch, seq)
            xs = _out_proj(xs, o, q_mem, k_mem, v_mem, (w_out_b, i), seq)
        xs = _ffn(xs, [(a, i) for a in ffn2])
    return xs.reshape(batch, seq, d)
```

```python
import functools

import jax
import jax.numpy as jnp
from jax import lax
from jax.experimental import pallas as pl
from jax.experimental.pallas import tpu as pltpu

F32 = jnp.float32
BF16 = jnp.bfloat16

EPS = 1e-6
GLA_HEADS = 4
GLA_GATE_RANK = 16
GLA_GATE_TAU = 16.0
GLA_CHUNK = 64
MLA_HEADS = 8
MLA_Q_RANK = 384
MLA_KV_RANK = 256
MLA_NOPE = 128
MLA_ROPE = 64
MLA_DV = 128
ROPE_BASE = 10000.0
MEM_HEADS = 4
MEM_DH = 128

LANES = 128
MLA_QK_PAD = 2 * LANES
VMEM_LIMIT_BYTES = 56 * 1024 * 1024

TOKEN_TILE = 512
SUBTILE_ROWS = 256
FLASH_TQ = 2048
FLASH_TK = 256
FLASH_CHAIN_ROWS = 256
FLASH_LOOKAHEAD = 16
ONES_ROWS = 16
GLA_LOOKAHEAD = 4

NEG_BIG = -1e30
LOG2_E = 1.4426950408889634


def _params(*sem):
    return pltpu.CompilerParams(dimension_semantics=sem, vmem_limit_bytes=VMEM_LIMIT_BYTES)


def _layer_spec(stacked, layer):
    nd = stacked.ndim - 1
    return pl.BlockSpec((None,) + stacked.shape[1:], lambda *_: (layer,) + (0,) * nd,
                        pipeline_mode=pl.Buffered(1))


def _layered(pairs):
    return [_layer_spec(a, l) for a, l in pairs], [a for a, _ in pairs]


def _rms(x, gain, denom=None):
    d = x.shape[-1] if denom is None else denom
    ms = jnp.sum(x * x, axis=-1, keepdims=True) * (1.0 / d)
    return x * lax.rsqrt(ms + EPS) * gain


def _dot(a, b):
    return jnp.dot(a, b, preferred_element_type=F32)


def _dot_nt(a, b):
    return lax.dot_general(a, b, (((1,), (1,)), ((), ())), preferred_element_type=F32)


def _dot_tn(a, b):
    return lax.dot_general(a, b, (((0,), (0,)), ((), ())), preferred_element_type=F32)


def _pipelined_subtiles(rows, matmuls, epilogue):
    sub = min(SUBTILE_ROWS, rows)
    slices = [slice(s * sub, (s + 1) * sub) for s in range(rows // sub)]
    staged = matmuls(slices[0])
    for s, rs in enumerate(slices):
        ready = staged
        if s + 1 < len(slices):
            staged = matmuls(slices[s + 1])
        epilogue(rs, *ready)


def _interleave(major, minor):
    per = -(-len(minor) // max(len(major), 1))
    for i, step in enumerate(major):
        step()
        for sub in minor[i * per:(i + 1) * per]:
            sub()
    for sub in minor[len(major) * per:]:
        sub()


def _ffn_body(x_ref, g_ref, wg_ref, wu_ref, wd_ref, o_ref):
    def matmuls(rs):
        h = _rms(x_ref[rs, :], g_ref[...]).astype(BF16)
        return _dot(h, wg_ref[...]), _dot(h, wu_ref[...])

    def epilogue(rs, gate, up):
        a = (gate * jax.nn.sigmoid(gate) * up).astype(BF16)
        o_ref[rs, :] = x_ref[rs, :] + 0.5 * _dot(a, wd_ref[...])

    _pipelined_subtiles(x_ref.shape[0], matmuls, epilogue)


def _ffn(x, params):
    t, d = x.shape
    tm = min(TOKEN_TILE, t)
    row = pl.BlockSpec((tm, d), lambda i: (i, 0))
    specs, ops = _layered(params)
    return pl.pallas_call(
        _ffn_body,
        grid=(t // tm,),
        in_specs=[row] + specs,
        out_specs=row,
        out_shape=jax.ShapeDtypeStruct((t, d), F32),
        compiler_params=_params("parallel"),
        name="ffn",
    )(x, *ops)


def _mem_q(p, gain):
    heads = [_rms(p[:, h * MEM_DH:(h + 1) * MEM_DH], gain) * (MEM_DH ** -0.5) for h in range(MEM_HEADS)]
    return jnp.concatenate(heads, axis=-1).astype(BF16)


def _mem_head(qm, km_ref, vm_ref, h):
    hs = slice(h * MEM_DH, (h + 1) * MEM_DH)
    s = _dot_nt(qm[:, hs], km_ref[:, hs])
    p = jnp.exp(s - jnp.max(s, axis=-1, keepdims=True))
    l = jnp.sum(p, axis=-1, keepdims=True)
    return (_dot(p.astype(BF16), vm_ref[:, hs]) / l).astype(BF16)


def _gla_layer_body(x_ref, km_ref, vm_ref, g_ref, w_ref, wa_ref, ba_ref, mqg_ref, gn_ref, wo_ref,
                    out_ref, s_ref, o_s, *, dk, dv):
    c = GLA_CHUNK
    rows = x_ref.shape[0]
    sub = min(SUBTILE_ROWS, rows)
    dqk = GLA_HEADS * dk
    dvt = GLA_HEADS * dv
    dmem = MEM_HEADS * MEM_DH

    @pl.when(pl.program_id(1) == 0)
    def _():
        s_ref[...] = jnp.zeros_like(s_ref)

    ri = lax.broadcasted_iota(jnp.int32, (c, c), 0)
    ci = lax.broadcasted_iota(jnp.int32, (c, c), 1)
    causal = ci <= ri
    tril = jnp.where(causal, 1.0, 0.0).astype(BF16)
    gn = gn_ref[...]

    def projection(rs, res):
        def col(lo, n):
            return _dot(res["h"], w_ref[:, lo:lo + n])

        def s_a():
            res["h"] = _rms(x_ref[rs, :], g_ref[...]).astype(BF16)
            a_low = col(2 * dqk + 2 * dvt + dmem, LANES).astype(BF16)
            z = _dot(a_low, wa_ref[...]) + ba_ref[...]
            log_sig = jnp.minimum(z, 0.0) - jnp.log1p(jnp.exp(-jnp.abs(z)))
            res["la"] = log_sig * (1.0 / GLA_GATE_TAU)

        def s_q():
            res["q"] = col(0, dqk).astype(BF16)

        def s_cum():
            la = res["la"]
            hi = la.astype(BF16)
            rest = la - hi.astype(F32)
            mid = rest.astype(BF16)
            lo = (rest - mid.astype(F32)).astype(BF16)
            res["bc"] = [sum(_dot(tril, t[ic * c:(ic + 1) * c, :]) for t in (hi, mid, lo))
                         for ic in range(sub // c)]

        def s_k():
            res["k"] = col(dqk, dqk).astype(BF16)

        def s_v():
            res["v"] = col(2 * dqk, dvt).astype(BF16)

        def s_r():
            res["r"] = col(2 * dqk + dvt, dvt).astype(BF16)

        def s_m():
            res["qm"] = _mem_q(col(2 * dqk + 2 * dvt, dmem), mqg_ref[...])

        return [s_a, s_q, s_cum, s_k, s_v, s_r, s_m]

    def scan(r0, res):
        items = [(ic, h) for ic in range(sub // c) for h in range(GLA_HEADS)]
        ahead = []

        def independent_part(item):
            ic, h = item
            cs = slice(ic * c, (ic + 1) * c)
            ks = slice(h * dk, (h + 1) * dk)
            bc = res["bc"][ic][:, ks]
            b_last = bc[c - 1:c, :]
            b_mid = bc[c // 2 - 1:c // 2, :]
            q = res["q"][cs, ks].astype(F32) * (dk ** -0.5)
            k = res["k"][cs, ks].astype(F32)
            v = res["v"][cs, h * dv:(h + 1) * dv]
            qe = (q * jnp.exp(bc - b_mid)).astype(BF16)
            ke = (k * jnp.exp(b_mid - bc)).astype(BF16)
            sc = _dot_nt(qe, ke)
            u = _dot_tn((k * jnp.exp(b_last - bc)).astype(BF16), v)
            qi = (q * jnp.exp(bc)).astype(BF16)
            dcol = jnp.exp(jnp.broadcast_to(b_last, (dk, dk))).T
            ahead.append((sc, u, qi, v, dcol))

        def dependent_part(item):
            ic, h = item
            sc, u, qi, v, dcol = ahead.pop(0)
            cs = slice(ic * c, (ic + 1) * c)
            vs = slice(h * dv, (h + 1) * dv)
            s_prev = s_ref[h]
            o = _dot(jnp.where(causal, sc, 0.0).astype(BF16), v) + _dot(qi, s_prev.astype(BF16))
            s_ref[h] = jnp.concatenate([dcol] * (dv // dk), axis=1) * s_prev + u
            r = res["r"][cs, vs].astype(F32)
            o_s[r0 + ic * c:r0 + (ic + 1) * c, vs] = (_rms(o, gn) * (r * jax.nn.sigmoid(r))).astype(BF16)

        steps = [functools.partial(independent_part, it) for it in items[:GLA_LOOKAHEAD]]
        for i, it in enumerate(items):
            if i + GLA_LOOKAHEAD < len(items):
                steps.append(functools.partial(independent_part, items[i + GLA_LOOKAHEAD]))
            steps.append(functools.partial(dependent_part, it))
        return steps

    def output(rs, res):
        heads = []
        acc = {}

        def mem_head(h):
            heads.append(_mem_head(res["qm"], km_ref, vm_ref, h))

        def mix():
            acc["y"] = _dot(o_s[rs, :], wo_ref[:dvt, :])

        def finish():
            y = acc["y"] + _dot(jnp.concatenate(heads, axis=-1), wo_ref[dvt:, :])
            out_ref[rs, :] = x_ref[rs, :] + y

        return [functools.partial(mem_head, h) for h in range(MEM_HEADS)] + [mix, finish]

    slices = [slice(s * sub, (s + 1) * sub) for s in range(rows // sub)]
    results = [{} for _ in slices]
    _interleave(projection(slices[0], results[0]), [])
    for s in range(len(slices)):
        majors = []
        if s + 1 < len(slices):
            majors += projection(slices[s + 1], results[s + 1])
        if s >= 1:
            majors += output(slices[s - 1], results[s - 1])
        _interleave(majors, scan(s * sub, results[s]))
    _interleave(output(slices[-1], results[-1]), [])


def _gla_layer(x, k_mem, v_mem, params, batch, seq, dk, dv):
    t, d = x.shape
    tm = min(TOKEN_TILE, seq)
    nb = seq // tm
    row = pl.BlockSpec((tm, d), lambda b, i: (b * nb + i, 0))
    mem_blk = pl.BlockSpec((None,) + k_mem.shape[1:], lambda b, i: (b, 0, 0))
    specs, ops = _layered(params)
    return pl.pallas_call(
        functools.partial(_gla_layer_body, dk=dk, dv=dv),
        grid=(batch, nb),
        in_specs=[row, mem_blk, mem_blk] + specs,
        out_specs=row,
        out_shape=jax.ShapeDtypeStruct((t, d), F32),
        scratch_shapes=[pltpu.VMEM((GLA_HEADS, dk, dv), F32), pltpu.VMEM((tm, GLA_HEADS * dv), BF16)],
        compiler_params=_params("parallel", "arbitrary"),
        name="gla_layer",
    )(x, k_mem, v_mem, *ops)


def _rope(x, cos, sin):
    return x * cos + pltpu.roll(x, LANES // 2, axis=1) * sin


def _ffn_mla_in_body(x_ref, fg_ref, wg_ref, wu_ref, wd_ref,
                     g_ref, w_ref, qn_ref, kvn_ref, wuq_ref, wuk_ref, wuvt_ref, qg_ref, kg_ref, krg_ref, mqg_ref,
                     cos_ref, sin_ref, x_out, q_out, k_out, vt_out, qm_out, pt_s, q_s, kn_s):
    o1 = MLA_Q_RANK
    o2 = o1 + MLA_KV_RANK
    qg = qg_ref[...]

    def ffn_in(rs):
        h = _rms(x_ref[rs, :], fg_ref[...]).astype(BF16)
        return _dot(h, wg_ref[...]), _dot(h, wu_ref[...])

    def ffn_out(rs, gate, up):
        a = (gate * jax.nn.sigmoid(gate) * up).astype(BF16)
        xn = x_ref[rs, :] + 0.5 * _dot(a, wd_ref[...])
        x_out[rs, :] = xn
        return xn

    def matmuls(xn):
        h = _rms(xn, g_ref[...]).astype(BF16)
        p = _dot(h, w_ref[...])
        cq = _rms(p[:, :o1], qn_ref[...]).astype(BF16)
        ckv = _rms(p[:, o1:o2], kvn_ref[...]).astype(BF16)
        return p[:, o2:], _dot(cq, wuq_ref[...]), _dot(ckv, wuk_ref[...]), _dot_nt(wuvt_ref[...], ckv)

    def epilogue(rs, p_tail, q, kn_all):
        qm_out[rs, :] = _mem_q(p_tail[:, LANES:], mqg_ref[...])
        cos = cos_ref[rs, :]
        sin = sin_ref[rs, :]
        k_rope = _rope(_rms(p_tail[:, :LANES], krg_ref[...], denom=MLA_ROPE), cos, sin).astype(BF16)
        for hd in range(MLA_HEADS):
            b0 = hd * MLA_QK_PAD
            qn = _rms(q[:, b0:b0 + LANES], qg[:, :LANES])
            qr = _rope(_rms(q[:, b0 + LANES:b0 + MLA_QK_PAD], qg[:, LANES:], denom=MLA_ROPE), cos, sin)
            q_out[rs, b0:b0 + LANES] = qn.astype(BF16)
            q_out[rs, b0 + LANES:b0 + MLA_QK_PAD] = qr.astype(BF16)
            kn = _rms(kn_all[:, hd * MLA_NOPE:(hd + 1) * MLA_NOPE], kg_ref[...])
            k_out[rs, b0:b0 + LANES] = kn.astype(BF16)
            k_out[rs, b0 + LANES:b0 + MLA_QK_PAD] = k_rope

    rows = x_ref.shape[0]
    sub = min(SUBTILE_ROWS, rows)
    slices = [slice(s * sub, (s + 1) * sub) for s in range(rows // sub)]

    @pl.when(pl.program_id(0) == 0)
    def _():
        pt_s[...] = jnp.zeros_like(pt_s)
        q_s[...] = jnp.zeros_like(q_s)
        kn_s[...] = jnp.zeros_like(kn_s)

    hidden = [ffn_in(slices[0])]
    for rs in slices:
        epilogue(rs, pt_s[rs, :], q_s[rs, :], kn_s[rs, :])
    hidden += [ffn_in(rs) for rs in slices[1:]]
    for rs, gate_up in zip(slices, hidden):
        p_tail, q, kn_all, vt = matmuls(ffn_out(rs, *gate_up))
        vt_out[:, rs] = vt.astype(BF16)
        pt_s[rs, :] = p_tail
        q_s[rs, :] = q
        kn_s[rs, :] = kn_all


def _ffn_mla_in(x, params, cos, sin, seq):
    t, d = x.shape
    tm = min(TOKEN_TILE, t, seq)
    nb = seq // tm
    last = t // tm - 1
    dq = MLA_HEADS * MLA_QK_PAD
    dvt = MLA_HEADS * MLA_DV
    dmem = MEM_HEADS * MEM_DH
    cur = lambda n: pl.BlockSpec((tm, n), lambda i: (jnp.minimum(i, last), 0))
    prev = lambda n: pl.BlockSpec((tm, n), lambda i: (jnp.maximum(i - 1, 0), 0))
    pos = pl.BlockSpec((tm, LANES), lambda i: (jnp.maximum(i - 1, 0) % nb, 0))
    specs, ops = _layered(params)
    p_tail = LANES + dmem
    return pl.pallas_call(
        _ffn_mla_in_body,
        grid=(t // tm + 1,),
        in_specs=[cur(d)] + specs + [pos, pos],
        out_specs=[cur(d), prev(dq), prev(dq), pl.BlockSpec((dvt, tm), lambda i: (0, jnp.minimum(i, last))),
                   prev(dmem)],
        out_shape=[jax.ShapeDtypeStruct((t, d), F32), jax.ShapeDtypeStruct((t, dq), BF16),
                   jax.ShapeDtypeStruct((t, dq), BF16), jax.ShapeDtypeStruct((dvt, t), BF16),
                   jax.ShapeDtypeStruct((t, dmem), BF16)],
        scratch_shapes=[pltpu.VMEM((tm, p_tail), F32), pltpu.VMEM((tm, dq), F32),
                        pltpu.VMEM((tm, MLA_HEADS * MLA_NOPE), F32)],
        compiler_params=_params("arbitrary"),
        name="ffn_mla_in",
    )(x, *ops, cos, sin)


def _flash_body(q_ref, k_ref, vt_ref, o_ref, *, tq, tk, rc):
    qi = pl.program_id(2)
    n_diag = tq // tk
    n_chain = tq // rc
    dv = vt_ref.shape[0]
    ones = jnp.ones((ONES_ROWS, tk), BF16)

    def run_blocks(j0, state, diagonal):
        items = []
        for d in range(n_diag):
            for c in range(n_chain):
                lo_q, hi_q = c * rc, (c + 1) * rc - 1
                lo_k, hi_k = d * tk, (d + 1) * tk - 1
                if diagonal and lo_k > hi_q:
                    continue
                masked = diagonal and hi_k > lo_q
                assert not masked or lo_k <= lo_q
                items.append((d, c, masked))
        blocks = {}

        def key_block(d):
            if d not in blocks:
                k0 = pl.multiple_of((j0 + d) * tk, tk)
                blocks[d] = (k_ref[pl.ds(k0, tk), :],
                             jnp.concatenate([vt_ref[:, pl.ds(k0, tk)], ones], axis=0))
            return blocks[d]

        def scores(item):
            d, c, _ = item
            return _dot_nt(key_block(d)[0], q_ref[c * rc:(c + 1) * rc, :])

        state = list(state)
        ahead = [scores(it) for it in items[:FLASH_LOOKAHEAD]]
        for i, (d, c, masked) in enumerate(items):
            if i + FLASH_LOOKAHEAD < len(items):
                ahead.append(scores(items[i + FLASH_LOOKAHEAD]))
            st = ahead.pop(0)
            if masked:
                kpos = d * tk + lax.broadcasted_iota(jnp.int32, (tk, rc), 0)
                qpos = c * rc + lax.broadcasted_iota(jnp.int32, (tk, rc), 1)
                st = jnp.where(kpos <= qpos, st, NEG_BIG)
            m, acc = state[c]
            m_new = jnp.maximum(m, jnp.max(st, axis=0, keepdims=True))
            alpha = jnp.exp2(m - m_new)
            pt = jnp.exp2(st - m_new).astype(BF16)
            state[c] = (m_new, alpha * acc + _dot(key_block(d)[1], pt))
        return tuple(state)

    init = tuple((jnp.full((1, rc), NEG_BIG, F32), jnp.zeros((dv + ONES_ROWS, rc), F32)) for _ in range(n_chain))
    state = lax.fori_loop(0, qi, lambda t, st: run_blocks(t * n_diag, st, False), init)
    state = run_blocks(qi * n_diag, state, True)
    for c in range(n_chain):
        acc = state[c][1]
        o_ref[c * rc:(c + 1) * rc, :] = (acc[:dv, :] / acc[dv:dv + 1, :]).T.astype(o_ref.dtype)


def _flash(q, k, vt, batch, seq):
    tq = min(FLASH_TQ, seq)
    tk = min(FLASH_TK, tq)
    rc = min(FLASH_CHAIN_ROWS, tq)
    q3 = q.reshape(batch, seq, -1)
    k3 = k.reshape(batch, seq, -1)
    out = pl.pallas_call(
        functools.partial(_flash_body, tq=tq, tk=tk, rc=rc),
        grid=(batch, MLA_HEADS, seq // tq),
        in_specs=[pl.BlockSpec((None, tq, MLA_QK_PAD), lambda b, h, i: (b, i, h)),
                  pl.BlockSpec((None, seq, MLA_QK_PAD), lambda b, h, i: (b, 0, h)),
                  pl.BlockSpec((MLA_DV, seq), lambda b, h, i: (h, b))],
        out_specs=pl.BlockSpec((None, tq, MLA_DV), lambda b, h, i: (b, i, h)),
        out_shape=jax.ShapeDtypeStruct((batch, seq, MLA_HEADS * MLA_DV), BF16),
        compiler_params=_params("parallel", "parallel", "arbitrary"),
        name="mla_flash",
    )(q3, k3, vt)
    return out.reshape(batch * seq, -1)


def _memkv_body(mem_ref, g_ref, w_ref, kg_ref, k_out, v_out):
    m = _rms(mem_ref[...], g_ref[...]).astype(BF16)
    kv = _dot(m, w_ref[...])
    dm = MEM_HEADS * MEM_DH
    heads = [_rms(kv[:, h * MEM_DH:(h + 1) * MEM_DH], kg_ref[...]) for h in range(MEM_HEADS)]
    k_out[...] = jnp.concatenate(heads, axis=-1).astype(BF16)
    v_out[...] = kv[:, dm:].astype(BF16)


def _memkv(mem, params):
    b, m, d = mem.shape
    specs, ops = _layered(params)
    dm = MEM_HEADS * MEM_DH
    blk = pl.BlockSpec((None, m, dm), lambda i: (i, 0, 0))
    return pl.pallas_call(
        _memkv_body,
        grid=(b,),
        in_specs=[pl.BlockSpec((None, m, d), lambda i: (i, 0, 0))] + specs,
        out_specs=[blk, blk],
        out_shape=[jax.ShapeDtypeStruct((b, m, dm), BF16)] * 2,
        compiler_params=_params("parallel"),
        name="mem_kv",
    )(mem, *ops)


def _out_body(x_ref, o_ref, qm_ref, km_ref, vm_ref, wo_ref, out_ref):
    qm = qm_ref[...]
    om = jnp.concatenate([_mem_head(qm, km_ref, vm_ref, h) for h in range(MEM_HEADS)], axis=-1)
    dmix = o_ref.shape[-1]
    y = _dot(o_ref[...], wo_ref[:dmix, :]) + _dot(om, wo_ref[dmix:, :])
    out_ref[...] = x_ref[...] + y


def _out_proj(x, o, q_mem, k_mem, v_mem, w_out, seq):
    t, d = x.shape
    tm = min(TOKEN_TILE, t, seq)
    nb = seq // tm
    row = lambda n: pl.BlockSpec((tm, n), lambda i: (i, 0))
    mem_blk = pl.BlockSpec((None,) + k_mem.shape[1:], lambda i: (i // nb, 0, 0))
    return pl.pallas_call(
        _out_body,
        grid=(t // tm,),
        in_specs=[row(d), row(o.shape[1]), row(q_mem.shape[1]), mem_blk, mem_blk, _layer_spec(*w_out)],
        out_specs=row(d),
        out_shape=jax.ShapeDtypeStruct((t, d), F32),
        compiler_params=_params("parallel"),
        name="out_proj",
    )(x, o, q_mem, k_mem, v_mem, w_out[0])


def _pad_rope(w):
    half = MLA_ROPE // 2
    z = jnp.zeros(w.shape[:-1] + (LANES // 2 - half,), w.dtype)
    return jnp.concatenate([w[..., :half], z, w[..., half:], z], axis=-1)


def _rope_tables(seq):
    inv = 1.0 / (ROPE_BASE ** (jnp.arange(0, MLA_ROPE, 2, dtype=F32) / MLA_ROPE))
    ang = jnp.arange(seq, dtype=F32)[:, None] * inv[None, :]
    cos = _pad_rope(jnp.concatenate([jnp.cos(ang)] * 2, axis=-1))
    sin = _pad_rope(jnp.concatenate([-jnp.sin(ang), jnp.sin(ang)], axis=-1))
    return cos, sin


def kernel(x, mem, ffn1_norm, ffn1_w_gate, ffn1_w_up, ffn1_w_down, ffn2_norm, ffn2_w_gate, ffn2_w_up, ffn2_w_down, mix_norm, w_out, mem_norm, mem_w_kv, memq_norm, memk_norm, gla_w_in, gla_w_alpha, gla_b_alpha, gla_out_norm, mla_w_in, mla_q_norm, mla_kv_norm, mla_w_uq, mla_w_ukv, mla_qn_norm, mla_qr_norm, mla_kn_norm, mla_kr_norm):
    batch, seq, d = x.shape
    depth = ffn1_norm.shape[0]
    t = batch * seq
    xs = x.reshape(t, d)
    cos, sin = _rope_tables(seq)
    gains = lambda a: a.astype(F32)[:, None, :]
    bf = lambda a: a.astype(BF16)

    ffn1 = [gains(ffn1_norm), bf(ffn1_w_gate), bf(ffn1_w_up), bf(ffn1_w_down)]
    ffn2 = [gains(ffn2_norm), bf(ffn2_w_gate), bf(ffn2_w_up), bf(ffn2_w_down)]
    mix_g, memq_g = gains(mix_norm), gains(memq_norm)
    memkv = [gains(mem_norm), bf(mem_w_kv), gains(memk_norm)]
    w_out_b = bf(w_out)

    gla_dqk = gla_w_alpha.shape[-1]
    gla_dv = d
    a_lo = 2 * gla_dqk + gla_dv
    a_hi = a_lo + GLA_GATE_RANK
    pad_rank = LANES - GLA_GATE_RANK
    gla_w = bf(jnp.concatenate([gla_w_in[..., :a_lo], gla_w_in[..., a_hi:],
                                jnp.pad(gla_w_in[..., a_lo:a_hi], ((0, 0), (0, 0), (0, pad_rank)))], axis=-1))
    gla_wa = bf(jnp.pad(gla_w_alpha, ((0, 0), (0, pad_rank), (0, 0))))
    gla_ba, gla_og = gains(gla_b_alpha), gains(gla_out_norm)

    r_lo = MLA_Q_RANK + MLA_KV_RANK
    r_hi = r_lo + MLA_ROPE
    n_mla = mla_w_in.shape[0]
    mla_w = bf(jnp.concatenate([mla_w_in[..., :r_lo], _pad_rope(mla_w_in[..., r_lo:r_hi]),
                                mla_w_in[..., r_hi:]], axis=-1))
    uq = mla_w_uq.reshape(n_mla, MLA_Q_RANK, MLA_HEADS, MLA_NOPE + MLA_ROPE)
    w_uq = bf(jnp.concatenate([uq[..., :MLA_NOPE], _pad_rope(uq[..., MLA_NOPE:])], axis=-1)
              .reshape(n_mla, MLA_Q_RANK, MLA_HEADS * MLA_QK_PAD))
    ukv = mla_w_ukv.reshape(n_mla, MLA_KV_RANK, MLA_HEADS, MLA_NOPE + MLA_DV)
    w_uk = bf(ukv[..., :MLA_NOPE].reshape(n_mla, MLA_KV_RANK, -1))
    w_uvt = bf(jnp.swapaxes(ukv[..., MLA_NOPE:].reshape(n_mla, MLA_KV_RANK, -1), 1, 2))
    qk_scale = (MLA_NOPE + MLA_ROPE) ** -0.5 * LOG2_E
    mla_qg = gains(jnp.concatenate([mla_qn_norm, _pad_rope(mla_qr_norm)], axis=-1) * qk_scale)
    mla_qn, mla_kvn = gains(mla_q_norm), gains(mla_kv_norm)
    mla_kg, mla_krg = gains(mla_kn_norm), gains(_pad_rope(mla_kr_norm))

    for i in range(depth):
        j = i // 2
        k_mem, v_mem = _memkv(mem, [(a, i) for a in memkv])
        if i % 2 == 0:
            xs = _ffn(xs, [(a, i) for a in ffn1])
            xs = _gla_layer(xs, k_mem, v_mem, [(mix_g, i), (gla_w, j), (gla_wa, j), (gla_ba, j), (memq_g, i),
                                               (gla_og, j), (w_out_b, i)],
                            batch, seq, gla_dqk // GLA_HEADS, gla_dv // GLA_HEADS)
        else:
            xs, q, k, vt, q_mem = _ffn_mla_in(
                xs, [(a, i) for a in ffn1] + [(mix_g, i), (mla_w, j), (mla_qn, j), (mla_kvn, j), (w_uq, j),
                                             (w_uk, j), (w_uvt, j), (mla_qg, j), (mla_kg, j), (mla_krg, j),
                                             (memq_g, i)], cos, sin, seq)
            o = _flash(q, k, vt, batch, seq)
            xs = _out_proj(xs, o, q_mem, k_mem, v_mem, (w_out_b, i), seq)
        xs = _ffn(xs, [(a, i) for a in ffn2])
    return xs.reshape(batch, seq, d)
```

```python
import functools

import jax
import jax.numpy as jnp
from jax import lax
from jax.experimental import pallas as pl
from jax.experimental.pallas import tpu as pltpu

F32 = jnp.float32
BF16 = jnp.bfloat16

EPS = 1e-6
GLA_HEADS = 4
GLA_GATE_RANK = 16
GLA_GATE_TAU = 16.0
GLA_CHUNK = 64
MLA_HEADS = 8
MLA_Q_RANK = 384
MLA_KV_RANK = 256
MLA_NOPE = 128
MLA_ROPE = 64
MLA_DV = 128
ROPE_BASE = 10000.0
MEM_HEADS = 4
MEM_DH = 128

LANES = 128
MLA_QK_PAD = 2 * LANES
VMEM_LIMIT_BYTES = 56 * 1024 * 1024

TOKEN_TILE = 512
SUBTILE_ROWS = 256
FLASH_TQ = 2048
FLASH_TK = 256
FLASH_CHAIN_ROWS = 256
FLASH_LOOKAHEAD = 16
ONES_ROWS = 16
GLA_LOOKAHEAD = 4

NEG_BIG = -1e30
LOG2_E = 1.4426950408889634


def _params(*sem):
    return pltpu.CompilerParams(dimension_semantics=sem, vmem_limit_bytes=VMEM_LIMIT_BYTES)


def _layer_spec(stacked, layer):
    nd = stacked.ndim - 1
    return pl.BlockSpec((None,) + stacked.shape[1:], lambda *_: (layer,) + (0,) * nd,
                        pipeline_mode=pl.Buffered(1))


def _layered(pairs):
    return [_layer_spec(a, l) for a, l in pairs], [a for a, _ in pairs]


def _rms(x, gain, denom=None):
    d = x.shape[-1] if denom is None else denom
    ms = jnp.sum(x * x, axis=-1, keepdims=True) * (1.0 / d)
    return x * lax.rsqrt(ms + EPS) * gain


def _dot(a, b):
    return jnp.dot(a, b, preferred_element_type=F32)


def _dot_nt(a, b):
    return lax.dot_general(a, b, (((1,), (1,)), ((), ())), preferred_element_type=F32)


def _dot_tn(a, b):
    return lax.dot_general(a, b, (((0,), (0,)), ((), ())), preferred_element_type=F32)


def _pipelined_subtiles(rows, matmuls, epilogue):
    sub = min(SUBTILE_ROWS, rows)
    slices = [slice(s * sub, (s + 1) * sub) for s in range(rows // sub)]
    staged = matmuls(slices[0])
    for s, rs in enumerate(slices):
        ready = staged
        if s + 1 < len(slices):
            staged = matmuls(slices[s + 1])
        epilogue(rs, *ready)


def _ffn_body(x_ref, g_ref, wg_ref, wu_ref, wd_ref, o_ref):
    def matmuls(rs):
        h = _rms(x_ref[rs, :], g_ref[...]).astype(BF16)
        return _dot(h, wg_ref[...]), _dot(h, wu_ref[...])

    def epilogue(rs, gate, up):
        a = (gate * jax.nn.sigmoid(gate) * up).astype(BF16)
        o_ref[rs, :] = x_ref[rs, :] + 0.5 * _dot(a, wd_ref[...])

    _pipelined_subtiles(x_ref.shape[0], matmuls, epilogue)


def _ffn(x, params):
    t, d = x.shape
    tm = min(TOKEN_TILE, t)
    row = pl.BlockSpec((tm, d), lambda i: (i, 0))
    specs, ops = _layered(params)
    return pl.pallas_call(
        _ffn_body,
        grid=(t // tm,),
        in_specs=[row] + specs,
        out_specs=row,
        out_shape=jax.ShapeDtypeStruct((t, d), F32),
        compiler_params=_params("parallel"),
        name="ffn",
    )(x, *ops)


def _mem_q(p, gain):
    heads = [_rms(p[:, h * MEM_DH:(h + 1) * MEM_DH], gain) * (MEM_DH ** -0.5) for h in range(MEM_HEADS)]
    return jnp.concatenate(heads, axis=-1).astype(BF16)


def _mem_head(qm, km_ref, vm_ref, h):
    hs = slice(h * MEM_DH, (h + 1) * MEM_DH)
    s = _dot_nt(qm[:, hs], km_ref[:, hs])
    p = jnp.exp(s - jnp.max(s, axis=-1, keepdims=True))
    l = jnp.sum(p, axis=-1, keepdims=True)
    return (_dot(p.astype(BF16), vm_ref[:, hs]) / l).astype(BF16)


def _gla_layer_body(x_ref, km_ref, vm_ref, g_ref, w_ref, wa_ref, ba_ref, mqg_ref, gn_ref, wo_ref,
                    out_ref, s_ref, o_s, *, dk, dv):
    c = GLA_CHUNK
    rows = x_ref.shape[0]
    sub = rows
    dqk = GLA_HEADS * dk
    dvt = GLA_HEADS * dv
    dmem = MEM_HEADS * MEM_DH

    @pl.when(pl.program_id(1) == 0)
    def _():
        s_ref[...] = jnp.zeros_like(s_ref)

    ri = lax.broadcasted_iota(jnp.int32, (c, c), 0)
    ci = lax.broadcasted_iota(jnp.int32, (c, c), 1)
    causal = ci <= ri
    tril = jnp.where(causal, 1.0, 0.0).astype(BF16)
    gn = gn_ref[...]

    def projection(rs, res):
        def col(lo, n):
            return _dot(res["h"], w_ref[:, lo:lo + n])

        def s_a():
            res["h"] = _rms(x_ref[rs, :], g_ref[...]).astype(BF16)
            a_low = col(2 * dqk + 2 * dvt + dmem, LANES).astype(BF16)
            z = _dot(a_low, wa_ref[...]) + ba_ref[...]
            log_sig = jnp.minimum(z, 0.0) - jnp.log1p(jnp.exp(-jnp.abs(z)))
            res["la"] = log_sig * (1.0 / GLA_GATE_TAU)

        def s_q():
            res["q"] = col(0, dqk).astype(BF16)

        def s_cum():
            la = res["la"]
            hi = la.astype(BF16)
            rest = la - hi.astype(F32)
            mid = rest.astype(BF16)
            lo = (rest - mid.astype(F32)).astype(BF16)
            res["bc"] = [sum(_dot(tril, t[ic * c:(ic + 1) * c, :]) for t in (hi, mid, lo))
                         for ic in range(sub // c)]

        def s_k():
            res["k"] = col(dqk, dqk).astype(BF16)

        def s_v():
            res["v"] = col(2 * dqk, dvt).astype(BF16)

        def s_r():
            res["r"] = col(2 * dqk + dvt, dvt).astype(BF16)

        def s_m():
            res["qm"] = _mem_q(col(2 * dqk + 2 * dvt, dmem), mqg_ref[...])

        return [s_a, s_q, s_cum, s_k, s_v, s_r, s_m]

    def scan():
        items = [(s, ic, h) for s in range(len(slices)) for ic in range(sub // c) for h in range(GLA_HEADS)]
        ahead = []

        def independent_part(item):
            s, ic, h = item
            res = results[s]
            cs = slice(ic * c, (ic + 1) * c)
            ks = slice(h * dk, (h + 1) * dk)
            bc = res["bc"][ic][:, ks]
            b_last = bc[c - 1:c, :]
            b_mid = bc[c // 2 - 1:c // 2, :]
            q = res["q"][cs, ks].astype(F32) * (dk ** -0.5)
            k = res["k"][cs, ks].astype(F32)
            v = res["v"][cs, h * dv:(h + 1) * dv]
            qe = (q * jnp.exp(bc - b_mid)).astype(BF16)
            ke = (k * jnp.exp(b_mid - bc)).astype(BF16)
            sc = _dot_nt(qe, ke)
            u = _dot_tn((k * jnp.exp(b_last - bc)).astype(BF16), v)
            qi = (q * jnp.exp(bc)).astype(BF16)
            dcol = jnp.exp(jnp.broadcast_to(b_last, (dk, dk))).T
            ahead.append((sc, u, qi, v, dcol))

        def dependent_part(item):
            s, ic, h = item
            res, r0 = results[s], s * sub
            sc, u, qi, v, dcol = ahead.pop(0)
            cs = slice(ic * c, (ic + 1) * c)
            vs = slice(h * dv, (h + 1) * dv)
            s_prev = s_ref[h]
            o = _dot(jnp.concatenate([qi, jnp.where(causal, sc, 0.0).astype(BF16)], axis=1),
                     jnp.concatenate([s_prev.astype(BF16), v], axis=0))
            s_ref[h] = jnp.concatenate([dcol] * (dv // dk), axis=1) * s_prev + u
            r = res["r"][cs, vs].astype(F32)
            o_s[r0 + ic * c:r0 + (ic + 1) * c, vs] = (_rms(o, gn) * (r * jax.nn.sigmoid(r))).astype(BF16)

        steps = [functools.partial(independent_part, it) for it in items[:GLA_LOOKAHEAD]]
        for i, it in enumerate(items):
            if i + GLA_LOOKAHEAD < len(items):
                steps.append(functools.partial(independent_part, items[i + GLA_LOOKAHEAD]))
            steps.append(functools.partial(dependent_part, it))
        return steps

    def output(rs, res):
        heads = []
        acc = {}

        def mem_head(h):
            heads.append(_mem_head(res["qm"], km_ref, vm_ref, h))

        def mix():
            acc["y"] = _dot(o_s[rs, :], wo_ref[:dvt, :])

        def finish():
            y = acc["y"] + _dot(jnp.concatenate(heads, axis=-1), wo_ref[dvt:, :])
            out_ref[rs, :] = x_ref[rs, :] + y

        return [functools.partial(mem_head, h) for h in range(MEM_HEADS)] + [mix, finish]

    slices = [slice(0, rows)]
    results = [{}]
    for step in projection(slices[0], results[0]) + scan() + output(slices[0], results[0]):
        step()


def _gla_layer(x, k_mem, v_mem, params, batch, seq, dk, dv):
    t, d = x.shape
    tm = min(TOKEN_TILE, seq)
    nb = seq // tm
    row = pl.BlockSpec((tm, d), lambda b, i: (b * nb + i, 0))
    mem_blk = pl.BlockSpec((None,) + k_mem.shape[1:], lambda b, i: (b, 0, 0))
    specs, ops = _layered(params)
    return pl.pallas_call(
        functools.partial(_gla_layer_body, dk=dk, dv=dv),
        grid=(batch, nb),
        in_specs=[row, mem_blk, mem_blk] + specs,
        out_specs=row,
        out_shape=jax.ShapeDtypeStruct((t, d), F32),
        scratch_shapes=[pltpu.VMEM((GLA_HEADS, dk, dv), F32), pltpu.VMEM((tm, GLA_HEADS * dv), BF16)],
        compiler_params=_params("parallel", "arbitrary"),
        name="gla_layer",
    )(x, k_mem, v_mem, *ops)


def _rope(x, cos, sin):
    return x * cos + pltpu.roll(x, LANES // 2, axis=1) * sin


def _ffn_mla_in_body(x_ref, fg_ref, wg_ref, wu_ref, wd_ref,
                     g_ref, w_ref, qn_ref, kvn_ref, wuq_ref, wuk_ref, wuvt_ref, qg_ref, kg_ref, krg_ref, mqg_ref,
                     cos_ref, sin_ref, x_out, q_out, k_out, vt_out, qm_out, pt_s, q_s, kn_s):
    o1 = MLA_Q_RANK
    o2 = o1 + MLA_KV_RANK
    qg = qg_ref[...]

    def ffn_in(rs):
        h = _rms(x_ref[rs, :], fg_ref[...]).astype(BF16)
        return _dot(h, wg_ref[...]), _dot(h, wu_ref[...])

    def ffn_out(rs, gate, up):
        a = (gate * jax.nn.sigmoid(gate) * up).astype(BF16)
        xn = x_ref[rs, :] + 0.5 * _dot(a, wd_ref[...])
        x_out[rs, :] = xn
        return xn

    def matmuls(xn):
        h = _rms(xn, g_ref[...]).astype(BF16)
        p = _dot(h, w_ref[...])
        cq = _rms(p[:, :o1], qn_ref[...]).astype(BF16)
        ckv = _rms(p[:, o1:o2], kvn_ref[...]).astype(BF16)
        return p[:, o2:], _dot(cq, wuq_ref[...]), _dot(ckv, wuk_ref[...]), _dot_nt(wuvt_ref[...], ckv)

    def epilogue(rs, p_tail, q, kn_all):
        qm_out[rs, :] = _mem_q(p_tail[:, LANES:], mqg_ref[...])
        cos = cos_ref[rs, :]
        sin = sin_ref[rs, :]
        k_rope = _rope(_rms(p_tail[:, :LANES], krg_ref[...], denom=MLA_ROPE), cos, sin).astype(BF16)
        for hd in range(MLA_HEADS):
            b0 = hd * MLA_QK_PAD
            qn = _rms(q[:, b0:b0 + LANES], qg[:, :LANES])
            qr = _rope(_rms(q[:, b0 + LANES:b0 + MLA_QK_PAD], qg[:, LANES:], denom=MLA_ROPE), cos, sin)
            q_out[rs, b0:b0 + LANES] = qn.astype(BF16)
            q_out[rs, b0 + LANES:b0 + MLA_QK_PAD] = qr.astype(BF16)
            kn = _rms(kn_all[:, hd * MLA_NOPE:(hd + 1) * MLA_NOPE], kg_ref[...])
            k_out[rs, b0:b0 + LANES] = kn.astype(BF16)
            k_out[rs, b0 + LANES:b0 + MLA_QK_PAD] = k_rope

    rows = x_ref.shape[0]
    sub = min(SUBTILE_ROWS, rows)
    slices = [slice(s * sub, (s + 1) * sub) for s in range(rows // sub)]

    @pl.when(pl.program_id(0) == 0)
    def _():
        pt_s[...] = jnp.zeros_like(pt_s)
        q_s[...] = jnp.zeros_like(q_s)
        kn_s[...] = jnp.zeros_like(kn_s)

    hidden = [ffn_in(slices[0])]
    for rs in slices:
        epilogue(rs, pt_s[rs, :], q_s[rs, :], kn_s[rs, :])
    hidden += [ffn_in(rs) for rs in slices[1:]]
    for rs, gate_up in zip(slices, hidden):
        p_tail, q, kn_all, vt = matmuls(ffn_out(rs, *gate_up))
        vt_out[:, rs] = vt.astype(BF16)
        pt_s[rs, :] = p_tail
        q_s[rs, :] = q
        kn_s[rs, :] = kn_all


def _ffn_mla_in(x, params, cos, sin, seq):
    t, d = x.shape
    tm = min(TOKEN_TILE, t, seq)
    nb = seq // tm
    last = t // tm - 1
    dq = MLA_HEADS * MLA_QK_PAD
    dvt = MLA_HEADS * MLA_DV
    dmem = MEM_HEADS * MEM_DH
    cur = lambda n: pl.BlockSpec((tm, n), lambda i: (jnp.minimum(i, last), 0))
    prev = lambda n: pl.BlockSpec((tm, n), lambda i: (jnp.maximum(i - 1, 0), 0))
    pos = pl.BlockSpec((tm, LANES), lambda i: (jnp.maximum(i - 1, 0) % nb, 0))
    specs, ops = _layered(params)
    p_tail = LANES + dmem
    return pl.pallas_call(
        _ffn_mla_in_body,
        grid=(t // tm + 1,),
        in_specs=[cur(d)] + specs + [pos, pos],
        out_specs=[cur(d), prev(dq), prev(dq), pl.BlockSpec((dvt, tm), lambda i: (0, jnp.minimum(i, last))),
                   prev(dmem)],
        out_shape=[jax.ShapeDtypeStruct((t, d), F32), jax.ShapeDtypeStruct((t, dq), BF16),
                   jax.ShapeDtypeStruct((t, dq), BF16), jax.ShapeDtypeStruct((dvt, t), BF16),
                   jax.ShapeDtypeStruct((t, dmem), BF16)],
        scratch_shapes=[pltpu.VMEM((tm, p_tail), F32), pltpu.VMEM((tm, dq), F32),
                        pltpu.VMEM((tm, MLA_HEADS * MLA_NOPE), F32)],
        compiler_params=_params("arbitrary"),
        name="ffn_mla_in",
    )(x, *ops, cos, sin)


def _flash_body(q_ref, k_ref, vt_ref, o_ref, *, tq, tk, rc):
    qi = pl.program_id(2)
    n_diag = tq // tk
    n_chain = tq // rc
    dv = vt_ref.shape[0]
    ones = jnp.ones((ONES_ROWS, tk), BF16)

    def run_blocks(j0, state, diagonal):
        items = []
        for d in range(n_diag):
            for c in range(n_chain):
                lo_q, hi_q = c * rc, (c + 1) * rc - 1
                lo_k, hi_k = d * tk, (d + 1) * tk - 1
                if diagonal and lo_k > hi_q:
                    continue
                masked = diagonal and hi_k > lo_q
                assert not masked or lo_k <= lo_q
                items.append((d, c, masked))
        blocks = {}

        def key_block(d):
            if d not in blocks:
                k0 = pl.multiple_of((j0 + d) * tk, tk)
                blocks[d] = (k_ref[pl.ds(k0, tk), :],
                             jnp.concatenate([vt_ref[:, pl.ds(k0, tk)], ones], axis=0))
            return blocks[d]

        def scores(item):
            d, c, _ = item
            return _dot_nt(key_block(d)[0], q_ref[c * rc:(c + 1) * rc, :])

        state = list(state)
        ahead = [scores(it) for it in items[:FLASH_LOOKAHEAD]]
        for i, (d, c, masked) in enumerate(items):
            if i + FLASH_LOOKAHEAD < len(items):
                ahead.append(scores(items[i + FLASH_LOOKAHEAD]))
            st = ahead.pop(0)
            if masked:
                kpos = d * tk + lax.broadcasted_iota(jnp.int32, (tk, rc), 0)
                qpos = c * rc + lax.broadcasted_iota(jnp.int32, (tk, rc), 1)
                st = jnp.where(kpos <= qpos, st, NEG_BIG)
            m, acc = state[c]
            m_new = jnp.maximum(m, jnp.max(st, axis=0, keepdims=True))
            alpha = jnp.exp2(m - m_new)
            pt = jnp.exp2(st - m_new).astype(BF16)
            state[c] = (m_new, alpha * acc + _dot(key_block(d)[1], pt))
        return tuple(state)

    init = tuple((jnp.full((1, rc), NEG_BIG, F32), jnp.zeros((dv + ONES_ROWS, rc), F32)) for _ in range(n_chain))
    state = lax.fori_loop(0, qi, lambda t, st: run_blocks(t * n_diag, st, False), init)
    state = run_blocks(qi * n_diag, state, True)
    for c in range(n_chain):
        acc = state[c][1]
        o_ref[c * rc:(c + 1) * rc, :] = (acc[:dv, :] / acc[dv:dv + 1, :]).T.astype(o_ref.dtype)


def _flash(q, k, vt, batch, seq):
    tq = min(FLASH_TQ, seq)
    tk = min(FLASH_TK, tq)
    rc = min(FLASH_CHAIN_ROWS, tq)
    q3 = q.reshape(batch, seq, -1)
    k3 = k.reshape(batch, seq, -1)
    out = pl.pallas_call(
        functools.partial(_flash_body, tq=tq, tk=tk, rc=rc),
        grid=(batch, MLA_HEADS, seq // tq),
        in_specs=[pl.BlockSpec((None, tq, MLA_QK_PAD), lambda b, h, i: (b, i, h)),
                  pl.BlockSpec((None, seq, MLA_QK_PAD), lambda b, h, i: (b, 0, h)),
                  pl.BlockSpec((MLA_DV, seq), lambda b, h, i: (h, b))],
        out_specs=pl.BlockSpec((None, tq, MLA_DV), lambda b, h, i: (b, i, h)),
        out_shape=jax.ShapeDtypeStruct((batch, seq, MLA_HEADS * MLA_DV), BF16),
        compiler_params=_params("parallel", "parallel", "arbitrary"),
        name="mla_flash",
    )(q3, k3, vt)
    return out.reshape(batch * seq, -1)


def _memkv_body(mem_ref, g_ref, w_ref, kg_ref, k_out, v_out):
    m = _rms(mem_ref[...], g_ref[...]).astype(BF16)
    kv = _dot(m, w_ref[...])
    dm = MEM_HEADS * MEM_DH
    heads = [_rms(kv[:, h * MEM_DH:(h + 1) * MEM_DH], kg_ref[...]) for h in range(MEM_HEADS)]
    k_out[...] = jnp.concatenate(heads, axis=-1).astype(BF16)
    v_out[...] = kv[:, dm:].astype(BF16)


def _memkv(mem, params):
    b, m, d = mem.shape
    specs, ops = _layered(params)
    dm = MEM_HEADS * MEM_DH
    blk = pl.BlockSpec((None, m, dm), lambda i: (i, 0, 0))
    return pl.pallas_call(
        _memkv_body,
        grid=(b,),
        in_specs=[pl.BlockSpec((None, m, d), lambda i: (i, 0, 0))] + specs,
        out_specs=[blk, blk],
        out_shape=[jax.ShapeDtypeStruct((b, m, dm), BF16)] * 2,
        compiler_params=_params("parallel"),
        name="mem_kv",
    )(mem, *ops)


def _out_body(x_ref, o_ref, qm_ref, km_ref, vm_ref, wo_ref, out_ref):
    qm = qm_ref[...]
    om = jnp.concatenate([_mem_head(qm, km_ref, vm_ref, h) for h in range(MEM_HEADS)], axis=-1)
    dmix = o_ref.shape[-1]
    y = _dot(o_ref[...], wo_ref[:dmix, :]) + _dot(om, wo_ref[dmix:, :])
    out_ref[...] = x_ref[...] + y


def _out_proj(x, o, q_mem, k_mem, v_mem, w_out, seq):
    t, d = x.shape
    tm = min(TOKEN_TILE, t, seq)
    nb = seq // tm
    row = lambda n: pl.BlockSpec((tm, n), lambda i: (i, 0))
    mem_blk = pl.BlockSpec((None,) + k_mem.shape[1:], lambda i: (i // nb, 0, 0))
    return pl.pallas_call(
        _out_body,
        grid=(t // tm,),
        in_specs=[row(d), row(o.shape[1]), row(q_mem.shape[1]), mem_blk, mem_blk, _layer_spec(*w_out)],
        out_specs=row(d),
        out_shape=jax.ShapeDtypeStruct((t, d), F32),
        compiler_params=_params("parallel"),
        name="out_proj",
    )(x, o, q_mem, k_mem, v_mem, w_out[0])


def _pad_rope(w):
    half = MLA_ROPE // 2
    z = jnp.zeros(w.shape[:-1] + (LANES // 2 - half,), w.dtype)
    return jnp.concatenate([w[..., :half], z, w[..., half:], z], axis=-1)


def _rope_tables(seq):
    inv = 1.0 / (ROPE_BASE ** (jnp.arange(0, MLA_ROPE, 2, dtype=F32) / MLA_ROPE))
    ang = jnp.arange(seq, dtype=F32)[:, None] * inv[None, :]
    cos = _pad_rope(jnp.concatenate([jnp.cos(ang)] * 2, axis=-1))
    sin = _pad_rope(jnp.concatenate([-jnp.sin(ang), jnp.sin(ang)], axis=-1))
    return cos, sin


def kernel(x, mem, ffn1_norm, ffn1_w_gate, ffn1_w_up, ffn1_w_down, ffn2_norm, ffn2_w_gate, ffn2_w_up, ffn2_w_down, mix_norm, w_out, mem_norm, mem_w_kv, memq_norm, memk_norm, gla_w_in, gla_w_alpha, gla_b_alpha, gla_out_norm, mla_w_in, mla_q_norm, mla_kv_norm, mla_w_uq, mla_w_ukv, mla_qn_norm, mla_qr_norm, mla_kn_norm, mla_kr_norm):
    batch, seq, d = x.shape
    depth = ffn1_norm.shape[0]
    t = batch * seq
    xs = x.reshape(t, d)
    cos, sin = _rope_tables(seq)
    gains = lambda a: a.astype(F32)[:, None, :]
    bf = lambda a: a.astype(BF16)

    ffn1 = [gains(ffn1_norm), bf(ffn1_w_gate), bf(ffn1_w_up), bf(ffn1_w_down)]
    ffn2 = [gains(ffn2_norm), bf(ffn2_w_gate), bf(ffn2_w_up), bf(ffn2_w_down)]
    mix_g, memq_g = gains(mix_norm), gains(memq_norm)
    memkv = [gains(mem_norm), bf(mem_w_kv), gains(memk_norm)]
    w_out_b = bf(w_out)

    gla_dqk = gla_w_alpha.shape[-1]
    gla_dv = d
    a_lo = 2 * gla_dqk + gla_dv
    a_hi = a_lo + GLA_GATE_RANK
    pad_rank = LANES - GLA_GATE_RANK
    gla_w = bf(jnp.concatenate([gla_w_in[..., :a_lo], gla_w_in[..., a_hi:],
                                jnp.pad(gla_w_in[..., a_lo:a_hi], ((0, 0), (0, 0), (0, pad_rank)))], axis=-1))
    gla_wa = bf(jnp.pad(gla_w_alpha, ((0, 0), (0, pad_rank), (0, 0))))
    gla_ba, gla_og = gains(gla_b_alpha), gains(gla_out_norm)

    r_lo = MLA_Q_RANK + MLA_KV_RANK
    r_hi = r_lo + MLA_ROPE
    n_mla = mla_w_in.shape[0]
    mla_w = bf(jnp.concatenate([mla_w_in[..., :r_lo], _pad_rope(mla_w_in[..., r_lo:r_hi]),
                                mla_w_in[..., r_hi:]], axis=-1))
    uq = mla_w_uq.reshape(n_mla, MLA_Q_RANK, MLA_HEADS, MLA_NOPE + MLA_ROPE)
    w_uq = bf(jnp.concatenate([uq[..., :MLA_NOPE], _pad_rope(uq[..., MLA_NOPE:])], axis=-1)
              .reshape(n_mla, MLA_Q_RANK, MLA_HEADS * MLA_QK_PAD))
    ukv = mla_w_ukv.reshape(n_mla, MLA_KV_RANK, MLA_HEADS, MLA_NOPE + MLA_DV)
    w_uk = bf(ukv[..., :MLA_NOPE].reshape(n_mla, MLA_KV_RANK, -1))
    w_uvt = bf(jnp.swapaxes(ukv[..., MLA_NOPE:].reshape(n_mla, MLA_KV_RANK, -1), 1, 2))
    qk_scale = (MLA_NOPE + MLA_ROPE) ** -0.5 * LOG2_E
    mla_qg = gains(jnp.concatenate([mla_qn_norm, _pad_rope(mla_qr_norm)], axis=-1) * qk_scale)
    mla_qn, mla_kvn = gains(mla_q_norm), gains(mla_kv_norm)
    mla_kg, mla_krg = gains(mla_kn_norm), gains(_pad_rope(mla_kr_norm))

    for i in range(depth):
        j = i // 2
        k_mem, v_mem = _memkv(mem, [(a, i) for a in memkv])
        if i % 2 == 0:
            xs = _ffn(xs, [(a, i) for a in ffn1])
            xs = _gla_layer(xs, k_mem, v_mem, [(mix_g, i), (gla_w, j), (gla_wa, j), (gla_ba, j), (memq_g, i),
                                               (gla_og, j), (w_out_b, i)],
                            batch, seq, gla_dqk // GLA_HEADS, gla_dv // GLA_HEADS)
        else:
            xs, q, k, vt, q_mem = _ffn_mla_in(
                xs, [(a, i) for a in ffn1] + [(mix_g, i), (mla_w, j), (mla_qn, j), (mla_kvn, j), (w_uq, j),
                                             (w_uk, j), (w_uvt, j), (mla_qg, j), (mla_kg, j), (mla_krg, j),
                                             (memq_g, i)], cos, sin, seq)
            o = _flash(q, k, vt, batch, seq)
            xs = _out_proj(xs, o, q_mem, k_mem, v_mem, (w_out_b, i), seq)
        xs = _ffn(xs, [(a, i) for a in ffn2])
    return xs.reshape(batch, seq, d)
```

```python
import functools

import jax
import jax.numpy as jnp
from jax import lax
from jax.experimental import pallas as pl
from jax.experimental.pallas import tpu as pltpu

F32 = jnp.float32
BF16 = jnp.bfloat16

EPS = 1e-6
GLA_HEADS = 4
GLA_GATE_RANK = 16
GLA_GATE_TAU = 16.0
GLA_CHUNK = 64
MLA_HEADS = 8
MLA_Q_RANK = 384
MLA_KV_RANK = 256
MLA_NOPE = 128
MLA_ROPE = 64
MLA_DV = 128
ROPE_BASE = 10000.0
MEM_HEADS = 4
MEM_DH = 128

LANES = 128
MLA_QK_PAD = 2 * LANES
VMEM_LIMIT_BYTES = 56 * 1024 * 1024

TOKEN_TILE = 512
SUBTILE_ROWS = 256
FLASH_TQ = 2048
FLASH_TK = 256
FLASH_CHAIN_ROWS = 256
FLASH_LOOKAHEAD = 16
CAST_ROW_ALIGN = 16
ONES_ROWS = 16
GLA_LOOKAHEAD = 4

NEG_BIG = -1e30
LOG2_E = 1.4426950408889634


def _params(*sem):
    return pltpu.CompilerParams(dimension_semantics=sem, vmem_limit_bytes=VMEM_LIMIT_BYTES)


def _layer_spec(stacked, layer):
    nd = stacked.ndim - 1
    return pl.BlockSpec((None,) + stacked.shape[1:], lambda *_: (layer,) + (0,) * nd,
                        pipeline_mode=pl.Buffered(1))


def _layered(pairs):
    return [_layer_spec(a, l) for a, l in pairs], [a for a, _ in pairs]


def _rms(x, gain, denom=None):
    d = x.shape[-1] if denom is None else denom
    ms = jnp.sum(x * x, axis=-1, keepdims=True) * (1.0 / d)
    return x * lax.rsqrt(ms + EPS) * gain


def _dot(a, b):
    return jnp.dot(a, b, preferred_element_type=F32)


def _dot_nt(a, b):
    return lax.dot_general(a, b, (((1,), (1,)), ((), ())), preferred_element_type=F32)


def _dot_tn(a, b):
    return lax.dot_general(a, b, (((0,), (0,)), ((), ())), preferred_element_type=F32)


def _pipelined_subtiles(rows, matmuls, epilogue):
    sub = min(SUBTILE_ROWS, rows)
    slices = [slice(s * sub, (s + 1) * sub) for s in range(rows // sub)]
    staged = matmuls(slices[0])
    for s, rs in enumerate(slices):
        ready = staged
        if s + 1 < len(slices):
            staged = matmuls(slices[s + 1])
        epilogue(rs, *ready)


def _cast_slabs(cast_next, steps):
    if cast_next is None:
        return [], [], [], []
    weights, layer = cast_next
    in_specs, out_specs, out_shapes = [], [], []
    for w in weights:
        _, r, c = w.shape
        slab = next(s for s in range(CAST_ROW_ALIGN, r + 1, CAST_ROW_ALIGN) if r % s == 0 and r // s <= steps)
        last = r // slab - 1
        in_specs.append(pl.BlockSpec((None, slab, c), lambda i, last=last: (layer, jnp.minimum(i, last), 0)))
        out_specs.append(pl.BlockSpec((slab, c), lambda i, last=last: (jnp.minimum(i, last), 0)))
        out_shapes.append(jax.ShapeDtypeStruct((r, c), BF16))
    return in_specs, list(weights), out_specs, out_shapes


def _cast_side_job(src_refs, dst_refs):
    for src, dst in zip(src_refs, dst_refs):
        dst[...] = src[...].astype(BF16)


def _ffn_body(x_ref, g_ref, wg_ref, wu_ref, wd_ref, *rest):
    n_cast = (len(rest) - 1) // 2
    o_ref = rest[n_cast]

    def matmuls(rs):
        h = _rms(x_ref[rs, :], g_ref[...]).astype(BF16)
        return _dot(h, wg_ref[...]), _dot(h, wu_ref[...])

    def epilogue(rs, gate, up):
        a = (gate * jax.nn.sigmoid(gate) * up).astype(BF16)
        o_ref[rs, :] = x_ref[rs, :] + 0.5 * _dot(a, wd_ref[...])

    _pipelined_subtiles(x_ref.shape[0], matmuls, epilogue)
    _cast_side_job(rest[:n_cast], rest[n_cast + 1:])


def _ffn(x, params, cast_next=None):
    t, d = x.shape
    tm = min(TOKEN_TILE, t)
    row = pl.BlockSpec((tm, d), lambda i: (i, 0))
    specs, ops = _layered(params)
    c_in, c_ops, c_out, c_shapes = _cast_slabs(cast_next, t // tm)
    out = pl.pallas_call(
        _ffn_body,
        grid=(t // tm,),
        in_specs=[row] + specs + c_in,
        out_specs=[row] + c_out,
        out_shape=[jax.ShapeDtypeStruct((t, d), F32)] + c_shapes,
        compiler_params=_params("arbitrary"),
        name="ffn",
    )(x, *ops, *c_ops)
    return out[0], out[1:]


def _mem_q(p, gain):
    heads = [_rms(p[:, h * MEM_DH:(h + 1) * MEM_DH], gain) * (MEM_DH ** -0.5) for h in range(MEM_HEADS)]
    return jnp.concatenate(heads, axis=-1).astype(BF16)


def _mem_head(qm, km_ref, vm_ref, h):
    hs = slice(h * MEM_DH, (h + 1) * MEM_DH)
    s = _dot_nt(qm[:, hs], km_ref[:, hs])
    p = jnp.exp(s - jnp.max(s, axis=-1, keepdims=True))
    l = jnp.sum(p, axis=-1, keepdims=True)
    return (_dot(p.astype(BF16), vm_ref[:, hs]) / l).astype(BF16)


def _gla_layer_body(x_ref, km_ref, vm_ref, g_ref, w_ref, wa_ref, ba_ref, mqg_ref, gn_ref, wo_ref,
                    out_ref, s_ref, o_s, *, dk, dv):
    c = GLA_CHUNK
    rows = x_ref.shape[0]
    sub = rows
    dqk = GLA_HEADS * dk
    dvt = GLA_HEADS * dv
    dmem = MEM_HEADS * MEM_DH

    @pl.when(pl.program_id(1) == 0)
    def _():
        s_ref[...] = jnp.zeros_like(s_ref)

    ri = lax.broadcasted_iota(jnp.int32, (c, c), 0)
    ci = lax.broadcasted_iota(jnp.int32, (c, c), 1)
    causal = ci <= ri
    tril = jnp.where(causal, 1.0, 0.0).astype(BF16)
    gn = gn_ref[...]

    def projection(rs, res):
        def col(lo, n):
            return _dot(res["h"], w_ref[:, lo:lo + n])

        def s_a():
            res["h"] = _rms(x_ref[rs, :], g_ref[...]).astype(BF16)
            a_low = col(2 * dqk + 2 * dvt + dmem, LANES).astype(BF16)
            z = _dot(a_low, wa_ref[...]) + ba_ref[...]
            log_sig = jnp.minimum(z, 0.0) - jnp.log1p(jnp.exp(-jnp.abs(z)))
            res["la"] = log_sig * (1.0 / GLA_GATE_TAU)

        def s_q():
            res["q"] = col(0, dqk).astype(BF16)

        def s_cum():
            la = res["la"]
            hi = la.astype(BF16)
            rest = la - hi.astype(F32)
            mid = rest.astype(BF16)
            lo = (rest - mid.astype(F32)).astype(BF16)
            res["bc"] = [sum(_dot(tril, t[ic * c:(ic + 1) * c, :]) for t in (hi, mid, lo))
                         for ic in range(sub // c)]

        def s_k():
            res["k"] = col(dqk, dqk).astype(BF16)

        def s_v():
            res["v"] = col(2 * dqk, dvt).astype(BF16)

        def s_r():
            res["r"] = col(2 * dqk + dvt, dvt).astype(BF16)

        def s_m():
            res["qm"] = _mem_q(col(2 * dqk + 2 * dvt, dmem), mqg_ref[...])

        return [s_a, s_q, s_cum, s_k, s_v, s_r, s_m]

    def scan():
        items = [(s, ic, h) for s in range(len(slices)) for ic in range(sub // c) for h in range(GLA_HEADS)]
        ahead = []

        def independent_part(item):
            s, ic, h = item
            res = results[s]
            cs = slice(ic * c, (ic + 1) * c)
            ks = slice(h * dk, (h + 1) * dk)
            bc = res["bc"][ic][:, ks]
            b_last = bc[c - 1:c, :]
            b_mid = bc[c // 2 - 1:c // 2, :]
            q = res["q"][cs, ks].astype(F32) * (dk ** -0.5)
            k = res["k"][cs, ks].astype(F32)
            v = res["v"][cs, h * dv:(h + 1) * dv]
            qe = (q * jnp.exp(bc - b_mid)).astype(BF16)
            ke = (k * jnp.exp(b_mid - bc)).astype(BF16)
            sc = _dot_nt(qe, ke)
            u = _dot_tn((k * jnp.exp(b_last - bc)).astype(BF16), v)
            qi = (q * jnp.exp(bc)).astype(BF16)
            dcol = jnp.exp(jnp.broadcast_to(b_last, (dk, dk))).T
            ahead.append((sc, u, qi, v, dcol))

        def dependent_part(item):
            s, ic, h = item
            res, r0 = results[s], s * sub
            sc, u, qi, v, dcol = ahead.pop(0)
            cs = slice(ic * c, (ic + 1) * c)
            vs = slice(h * dv, (h + 1) * dv)
            s_prev = s_ref[h]
            o = _dot(jnp.concatenate([qi, jnp.where(causal, sc, 0.0).astype(BF16)], axis=1),
                     jnp.concatenate([s_prev.astype(BF16), v], axis=0))
            s_ref[h] = jnp.concatenate([dcol] * (dv // dk), axis=1) * s_prev + u
            r = res["r"][cs, vs].astype(F32)
            o_s[r0 + ic * c:r0 + (ic + 1) * c, vs] = (_rms(o, gn) * (r * jax.nn.sigmoid(r))).astype(BF16)

        steps = [functools.partial(independent_part, it) for it in items[:GLA_LOOKAHEAD]]
        for i, it in enumerate(items):
            if i + GLA_LOOKAHEAD < len(items):
                steps.append(functools.partial(independent_part, items[i + GLA_LOOKAHEAD]))
            steps.append(functools.partial(dependent_part, it))
        return steps

    def output(rs, res):
        heads = []
        acc = {}

        def mem_head(h):
            heads.append(_mem_head(res["qm"], km_ref, vm_ref, h))

        def mix():
            acc["y"] = _dot(o_s[rs, :], wo_ref[:dvt, :])

        def finish():
            y = acc["y"] + _dot(jnp.concatenate(heads, axis=-1), wo_ref[dvt:, :])
            out_ref[rs, :] = x_ref[rs, :] + y

        return [functools.partial(mem_head, h) for h in range(MEM_HEADS)] + [mix, finish]

    slices = [slice(0, rows)]
    results = [{}]
    for step in projection(slices[0], results[0]) + scan() + output(slices[0], results[0]):
        step()


def _gla_layer(x, k_mem, v_mem, params, batch, seq, dk, dv):
    t, d = x.shape
    tm = min(TOKEN_TILE, seq)
    nb = seq // tm
    row = pl.BlockSpec((tm, d), lambda b, i: (b * nb + i, 0))
    mem_blk = pl.BlockSpec((None,) + k_mem.shape[1:], lambda b, i: (b, 0, 0))
    specs, ops = _layered(params)
    return pl.pallas_call(
        functools.partial(_gla_layer_body, dk=dk, dv=dv),
        grid=(batch, nb),
        in_specs=[row, mem_blk, mem_blk] + specs,
        out_specs=row,
        out_shape=jax.ShapeDtypeStruct((t, d), F32),
        scratch_shapes=[pltpu.VMEM((GLA_HEADS, dk, dv), F32), pltpu.VMEM((tm, GLA_HEADS * dv), BF16)],
        compiler_params=_params("parallel", "arbitrary"),
        name="gla_layer",
    )(x, k_mem, v_mem, *ops)


def _rope(x, cos, sin):
    return x * cos + pltpu.roll(x, LANES // 2, axis=1) * sin


def _ffn_mla_in_body(x_ref, fg_ref, wg_ref, wu_ref, wd_ref,
                     g_ref, w_ref, qn_ref, kvn_ref, wuq_ref, wuk_ref, wuvt_ref, qg_ref, kg_ref, krg_ref, mqg_ref,
                     cos_ref, sin_ref, *rest):
    n_cast = (len(rest) - 8) // 2
    x_out, q_out, k_out, vt_out, qm_out = rest[n_cast:n_cast + 5]
    pt_s, q_s, kn_s = rest[-3:]
    o1 = MLA_Q_RANK
    o2 = o1 + MLA_KV_RANK
    qg = qg_ref[...]

    def ffn_in(rs):
        h = _rms(x_ref[rs, :], fg_ref[...]).astype(BF16)
        return _dot(h, wg_ref[...]), _dot(h, wu_ref[...])

    def ffn_out(rs, gate, up):
        a = (gate * jax.nn.sigmoid(gate) * up).astype(BF16)
        xn = x_ref[rs, :] + 0.5 * _dot(a, wd_ref[...])
        x_out[rs, :] = xn
        return xn

    def matmuls(xn):
        h = _rms(xn, g_ref[...]).astype(BF16)
        p = _dot(h, w_ref[...])
        cq = _rms(p[:, :o1], qn_ref[...]).astype(BF16)
        ckv = _rms(p[:, o1:o2], kvn_ref[...]).astype(BF16)
        return p[:, o2:], _dot(cq, wuq_ref[...]), _dot(ckv, wuk_ref[...]), _dot_nt(wuvt_ref[...], ckv)

    def epilogue(rs, p_tail, q, kn_all):
        qm_out[rs, :] = _mem_q(p_tail[:, LANES:], mqg_ref[...])
        cos = cos_ref[rs, :]
        sin = sin_ref[rs, :]
        k_rope = _rope(_rms(p_tail[:, :LANES], krg_ref[...], denom=MLA_ROPE), cos, sin).astype(BF16)
        for hd in range(MLA_HEADS):
            b0 = hd * MLA_QK_PAD
            qn = _rms(q[:, b0:b0 + LANES], qg[:, :LANES])
            qr = _rope(_rms(q[:, b0 + LANES:b0 + MLA_QK_PAD], qg[:, LANES:], denom=MLA_ROPE), cos, sin)
            q_out[rs, b0:b0 + LANES] = qn.astype(BF16)
            q_out[rs, b0 + LANES:b0 + MLA_QK_PAD] = qr.astype(BF16)
            kn = _rms(kn_all[:, hd * MLA_NOPE:(hd + 1) * MLA_NOPE], kg_ref[...])
            k_out[rs, b0:b0 + LANES] = kn.astype(BF16)
            k_out[rs, b0 + LANES:b0 + MLA_QK_PAD] = k_rope

    rows = x_ref.shape[0]
    sub = min(SUBTILE_ROWS, rows)
    slices = [slice(s * sub, (s + 1) * sub) for s in range(rows // sub)]

    @pl.when(pl.program_id(0) == 0)
    def _():
        pt_s[...] = jnp.zeros_like(pt_s)
        q_s[...] = jnp.zeros_like(q_s)
        kn_s[...] = jnp.zeros_like(kn_s)

    hidden = [ffn_in(slices[0])]
    for rs in slices:
        epilogue(rs, pt_s[rs, :], q_s[rs, :], kn_s[rs, :])
    hidden += [ffn_in(rs) for rs in slices[1:]]
    for rs, gate_up in zip(slices, hidden):
        p_tail, q, kn_all, vt = matmuls(ffn_out(rs, *gate_up))
        vt_out[:, rs] = vt.astype(BF16)
        pt_s[rs, :] = p_tail
        q_s[rs, :] = q
        kn_s[rs, :] = kn_all
    _cast_side_job(rest[:n_cast], rest[n_cast + 5:-3])


def _ffn_mla_in(x, params, cos, sin, seq, cast_next=None):
    t, d = x.shape
    tm = min(TOKEN_TILE, t, seq)
    nb = seq // tm
    last = t // tm - 1
    dq = MLA_HEADS * MLA_QK_PAD
    dvt = MLA_HEADS * MLA_DV
    dmem = MEM_HEADS * MEM_DH
    cur = lambda n: pl.BlockSpec((tm, n), lambda i: (jnp.minimum(i, last), 0))
    prev = lambda n: pl.BlockSpec((tm, n), lambda i: (jnp.maximum(i - 1, 0), 0))
    pos = pl.BlockSpec((tm, LANES), lambda i: (jnp.maximum(i - 1, 0) % nb, 0))
    specs, ops = _layered(params)
    p_tail = LANES + dmem
    c_in, c_ops, c_out, c_shapes = _cast_slabs(cast_next, t // tm + 1)
    out = pl.pallas_call(
        _ffn_mla_in_body,
        grid=(t // tm + 1,),
        in_specs=[cur(d)] + specs + [pos, pos] + c_in,
        out_specs=[cur(d), prev(dq), prev(dq), pl.BlockSpec((dvt, tm), lambda i: (0, jnp.minimum(i, last))),
                   prev(dmem)] + c_out,
        out_shape=[jax.ShapeDtypeStruct((t, d), F32), jax.ShapeDtypeStruct((t, dq), BF16),
                   jax.ShapeDtypeStruct((t, dq), BF16), jax.ShapeDtypeStruct((dvt, t), BF16),
                   jax.ShapeDtypeStruct((t, dmem), BF16)] + c_shapes,
        scratch_shapes=[pltpu.VMEM((tm, p_tail), F32), pltpu.VMEM((tm, dq), F32),
                        pltpu.VMEM((tm, MLA_HEADS * MLA_NOPE), F32)],
        compiler_params=_params("arbitrary"),
        name="ffn_mla_in",
    )(x, *ops, cos, sin, *c_ops)
    return out[:5], out[5:]


def _flash_body(q_ref, k_ref, vt_ref, o_ref, *, tq, tk, rc):
    qi = pl.program_id(2)
    n_diag = tq // tk
    n_chain = tq // rc
    dv = vt_ref.shape[0]
    ones = jnp.ones((ONES_ROWS, tk), BF16)

    def run_blocks(j0, state, diagonal):
        items = []
        for d in range(n_diag):
            for c in range(n_chain):
                lo_q, hi_q = c * rc, (c + 1) * rc - 1
                lo_k, hi_k = d * tk, (d + 1) * tk - 1
                if diagonal and lo_k > hi_q:
                    continue
                masked = diagonal and hi_k > lo_q
                assert not masked or lo_k <= lo_q
                items.append((d, c, masked))
        blocks = {}

        def key_block(d):
            if d not in blocks:
                k0 = pl.multiple_of((j0 + d) * tk, tk)
                blocks[d] = (k_ref[pl.ds(k0, tk), :],
                             jnp.concatenate([vt_ref[:, pl.ds(k0, tk)], ones], axis=0))
            return blocks[d]

        def scores(item):
            d, c, _ = item
            return _dot_nt(key_block(d)[0], q_ref[c * rc:(c + 1) * rc, :])

        state = list(state)
        ahead = [scores(it) for it in items[:FLASH_LOOKAHEAD]]
        for i, (d, c, masked) in enumerate(items):
            if i + FLASH_LOOKAHEAD < len(items):
                ahead.append(scores(items[i + FLASH_LOOKAHEAD]))
            st = ahead.pop(0)
            if masked:
                kpos = d * tk + lax.broadcasted_iota(jnp.int32, (tk, rc), 0)
                qpos = c * rc + lax.broadcasted_iota(jnp.int32, (tk, rc), 1)
                st = jnp.where(kpos <= qpos, st, NEG_BIG)
            m, acc = state[c]
            m_new = jnp.maximum(m, jnp.max(st, axis=0, keepdims=True))
            alpha = jnp.exp2(m - m_new)
            pt = jnp.exp2(st - m_new).astype(BF16)
            state[c] = (m_new, alpha * acc + _dot(key_block(d)[1], pt))
        return tuple(state)

    init = tuple((jnp.full((1, rc), NEG_BIG, F32), jnp.zeros((dv + ONES_ROWS, rc), F32)) for _ in range(n_chain))
    state = lax.fori_loop(0, qi, lambda t, st: run_blocks(t * n_diag, st, False), init)
    state = run_blocks(qi * n_diag, state, True)
    for c in range(n_chain):
        acc = state[c][1]
        o_ref[c * rc:(c + 1) * rc, :] = (acc[:dv, :] / acc[dv:dv + 1, :]).T.astype(o_ref.dtype)


def _flash(q, k, vt, batch, seq):
    tq = min(FLASH_TQ, seq)
    tk = min(FLASH_TK, tq)
    rc = min(FLASH_CHAIN_ROWS, tq)
    q3 = q.reshape(batch, seq, -1)
    k3 = k.reshape(batch, seq, -1)
    out = pl.pallas_call(
        functools.partial(_flash_body, tq=tq, tk=tk, rc=rc),
        grid=(batch, MLA_HEADS, seq // tq),
        in_specs=[pl.BlockSpec((None, tq, MLA_QK_PAD), lambda b, h, i: (b, i, h)),
                  pl.BlockSpec((None, seq, MLA_QK_PAD), lambda b, h, i: (b, 0, h)),
                  pl.BlockSpec((MLA_DV, seq), lambda b, h, i: (h, b))],
        out_specs=pl.BlockSpec((None, tq, MLA_DV), lambda b, h, i: (b, i, h)),
        out_shape=jax.ShapeDtypeStruct((batch, seq, MLA_HEADS * MLA_DV), BF16),
        compiler_params=_params("parallel", "parallel", "arbitrary"),
        name="mla_flash",
    )(q3, k3, vt)
    return out.reshape(batch * seq, -1)


def _memkv_body(mem_ref, g_ref, w_ref, kg_ref, k_out, v_out):
    m = _rms(mem_ref[...], g_ref[...]).astype(BF16)
    kv = _dot(m, w_ref[...])
    dm = MEM_HEADS * MEM_DH
    heads = [_rms(kv[:, h * MEM_DH:(h + 1) * MEM_DH], kg_ref[...]) for h in range(MEM_HEADS)]
    k_out[...] = jnp.concatenate(heads, axis=-1).astype(BF16)
    v_out[...] = kv[:, dm:].astype(BF16)


def _memkv(mem, params):
    b, m, d = mem.shape
    specs, ops = _layered(params)
    dm = MEM_HEADS * MEM_DH
    blk = pl.BlockSpec((None, m, dm), lambda i: (i, 0, 0))
    return pl.pallas_call(
        _memkv_body,
        grid=(b,),
        in_specs=[pl.BlockSpec((None, m, d), lambda i: (i, 0, 0))] + specs,
        out_specs=[blk, blk],
        out_shape=[jax.ShapeDtypeStruct((b, m, dm), BF16)] * 2,
        compiler_params=_params("parallel"),
        name="mem_kv",
    )(mem, *ops)


def _out_body(x_ref, o_ref, qm_ref, km_ref, vm_ref, wo_ref, out_ref):
    qm = qm_ref[...]
    om = jnp.concatenate([_mem_head(qm, km_ref, vm_ref, h) for h in range(MEM_HEADS)], axis=-1)
    dmix = o_ref.shape[-1]
    y = _dot(o_ref[...], wo_ref[:dmix, :]) + _dot(om, wo_ref[dmix:, :])
    out_ref[...] = x_ref[...] + y


def _out_proj(x, o, q_mem, k_mem, v_mem, w_out, seq):
    t, d = x.shape
    tm = min(TOKEN_TILE, t, seq)
    nb = seq // tm
    row = lambda n: pl.BlockSpec((tm, n), lambda i: (i, 0))
    mem_blk = pl.BlockSpec((None,) + k_mem.shape[1:], lambda i: (i // nb, 0, 0))
    return pl.pallas_call(
        _out_body,
        grid=(t // tm,),
        in_specs=[row(d), row(o.shape[1]), row(q_mem.shape[1]), mem_blk, mem_blk, _layer_spec(*w_out)],
        out_specs=row(d),
        out_shape=jax.ShapeDtypeStruct((t, d), F32),
        compiler_params=_params("parallel"),
        name="out_proj",
    )(x, o, q_mem, k_mem, v_mem, w_out[0])


def _pad_rope(w):
    half = MLA_ROPE // 2
    z = jnp.zeros(w.shape[:-1] + (LANES // 2 - half,), w.dtype)
    return jnp.concatenate([w[..., :half], z, w[..., half:], z], axis=-1)


def _rope_tables(seq):
    inv = 1.0 / (ROPE_BASE ** (jnp.arange(0, MLA_ROPE, 2, dtype=F32) / MLA_ROPE))
    ang = jnp.arange(seq, dtype=F32)[:, None] * inv[None, :]
    cos = _pad_rope(jnp.concatenate([jnp.cos(ang)] * 2, axis=-1))
    sin = _pad_rope(jnp.concatenate([-jnp.sin(ang), jnp.sin(ang)], axis=-1))
    return cos, sin


def kernel(x, mem, ffn1_norm, ffn1_w_gate, ffn1_w_up, ffn1_w_down, ffn2_norm, ffn2_w_gate, ffn2_w_up, ffn2_w_down, mix_norm, w_out, mem_norm, mem_w_kv, memq_norm, memk_norm, gla_w_in, gla_w_alpha, gla_b_alpha, gla_out_norm, mla_w_in, mla_q_norm, mla_kv_norm, mla_w_uq, mla_w_ukv, mla_qn_norm, mla_qr_norm, mla_kn_norm, mla_kr_norm):
    batch, seq, d = x.shape
    depth = ffn1_norm.shape[0]
    t = batch * seq
    xs = x.reshape(t, d)
    cos, sin = _rope_tables(seq)
    gains = lambda a: a.astype(F32)[:, None, :]
    bf = lambda a: a.astype(BF16)

    ffn_g = [gains(ffn1_norm), gains(ffn2_norm)]
    ffn_w = [[ffn1_w_gate, ffn1_w_up, ffn1_w_down], [ffn2_w_gate, ffn2_w_up, ffn2_w_down]]
    ffn_calls = [(half, i) for i in range(depth) for half in (0, 1)]
    ffn_bf = [bf(w[:1]) for w in ffn_w[0]]
    mix_g, memq_g = gains(mix_norm), gains(memq_norm)
    side = [w.reshape(1, -1, w.shape[-1]) for w in (w_out, mem_w_kv)]

    gla_dqk = gla_w_alpha.shape[-1]
    gla_dv = d
    a_lo = 2 * gla_dqk + gla_dv
    a_hi = a_lo + GLA_GATE_RANK
    pad_rank = LANES - GLA_GATE_RANK
    gla_w = bf(jnp.concatenate([gla_w_in[..., :a_lo], gla_w_in[..., a_hi:],
                                jnp.pad(gla_w_in[..., a_lo:a_hi], ((0, 0), (0, 0), (0, pad_rank)))], axis=-1))
    gla_wa = bf(jnp.pad(gla_w_alpha, ((0, 0), (0, pad_rank), (0, 0))))
    gla_ba, gla_og = gains(gla_b_alpha), gains(gla_out_norm)

    r_lo = MLA_Q_RANK + MLA_KV_RANK
    r_hi = r_lo + MLA_ROPE
    n_mla = mla_w_in.shape[0]
    mla_w = bf(jnp.concatenate([mla_w_in[..., :r_lo], _pad_rope(mla_w_in[..., r_lo:r_hi]),
                                mla_w_in[..., r_hi:]], axis=-1))
    uq = mla_w_uq.reshape(n_mla, MLA_Q_RANK, MLA_HEADS, MLA_NOPE + MLA_ROPE)
    w_uq = bf(jnp.concatenate([uq[..., :MLA_NOPE], _pad_rope(uq[..., MLA_NOPE:])], axis=-1)
              .reshape(n_mla, MLA_Q_RANK, MLA_HEADS * MLA_QK_PAD))
    ukv = mla_w_ukv.reshape(n_mla, MLA_KV_RANK, MLA_HEADS, MLA_NOPE + MLA_DV)
    w_uk = bf(ukv[..., :MLA_NOPE].reshape(n_mla, MLA_KV_RANK, -1))
    w_uvt = bf(jnp.swapaxes(ukv[..., MLA_NOPE:].reshape(n_mla, MLA_KV_RANK, -1), 1, 2))
    qk_scale = (MLA_NOPE + MLA_ROPE) ** -0.5 * LOG2_E
    mla_qg = gains(jnp.concatenate([mla_qn_norm, _pad_rope(mla_qr_norm)], axis=-1) * qk_scale)
    mla_qn, mla_kvn = gains(mla_q_norm), gains(mla_kv_norm)
    mla_kg, mla_krg = gains(mla_kn_norm), gains(_pad_rope(mla_kr_norm))

    def ffn_args(half, i):
        n = ffn_calls.index((half, i)) + 1
        nxt = (ffn_w[ffn_calls[n][0]], ffn_calls[n][1]) if n < len(ffn_calls) else None
        return [(ffn_g[half], i)] + [(w[None] if w.ndim == 2 else w, 0) for w in ffn_bf], nxt

    for i in range(depth):
        j = i // 2
        params, nxt = ffn_args(0, i)
        if i == 0:
            xs, cast = _ffn(xs, params, (nxt[0] + side, 0))
            ffn_bf = cast[:len(nxt[0])]
            w_out_b, mem_w_b = [c.reshape(w.shape) for c, w in zip(cast[len(nxt[0]):], (w_out, mem_w_kv))]
            memkv = [gains(mem_norm), mem_w_b, gains(memk_norm)]
        elif i % 2 == 0:
            xs, ffn_bf = _ffn(xs, params, nxt)
        k_mem, v_mem = _memkv(mem, [(a, i) for a in memkv])
        if i % 2 == 0:
            xs = _gla_layer(xs, k_mem, v_mem, [(mix_g, i), (gla_w, j), (gla_wa, j), (gla_ba, j), (memq_g, i),
                                               (gla_og, j), (w_out_b, i)],
                            batch, seq, gla_dqk // GLA_HEADS, gla_dv // GLA_HEADS)
        else:
            (xs, q, k, vt, q_mem), ffn_bf = _ffn_mla_in(
                xs, params + [(mix_g, i), (mla_w, j), (mla_qn, j), (mla_kvn, j), (w_uq, j), (w_uk, j),
                              (w_uvt, j), (mla_qg, j), (mla_kg, j), (mla_krg, j), (memq_g, i)],
                cos, sin, seq, nxt)
            o = _flash(q, k, vt, batch, seq)
            xs = _out_proj(xs, o, q_mem, k_mem, v_mem, (w_out_b, i), seq)
        params, nxt = ffn_args(1, i)
        xs, ffn_bf = _ffn(xs, params, nxt)
    return xs.reshape(batch, seq, d)
```

```python
import functools

import jax
import jax.numpy as jnp
from jax import lax
from jax.experimental import pallas as pl
from jax.experimental.pallas import tpu as pltpu

F32 = jnp.float32
BF16 = jnp.bfloat16

EPS = 1e-6
GLA_HEADS = 4
GLA_GATE_RANK = 16
GLA_GATE_TAU = 16.0
GLA_CHUNK = 64
MLA_HEADS = 8
MLA_Q_RANK = 384
MLA_KV_RANK = 256
MLA_NOPE = 128
MLA_ROPE = 64
MLA_DV = 128
ROPE_BASE = 10000.0
MEM_HEADS = 4
MEM_DH = 128

LANES = 128
MLA_QK_PAD = 2 * LANES
VMEM_LIMIT_BYTES = 56 * 1024 * 1024

TOKEN_TILE = 512
FFN_TILE = 1024
SUBTILE_ROWS = 256
FLASH_TQ = 2048
FLASH_TK = 256
FLASH_CHAIN_ROWS = 256
FLASH_LOOKAHEAD = 16
CAST_ROW_ALIGN = 16
ONES_ROWS = 16
GLA_LOOKAHEAD = 4

NEG_BIG = -1e30
LOG2_E = 1.4426950408889634


def _params(*sem):
    return pltpu.CompilerParams(dimension_semantics=sem, vmem_limit_bytes=VMEM_LIMIT_BYTES)


def _layer_spec(stacked, layer):
    nd = stacked.ndim - 1
    return pl.BlockSpec((None,) + stacked.shape[1:], lambda *_: (layer,) + (0,) * nd,
                        pipeline_mode=pl.Buffered(1))


def _layered(pairs):
    return [_layer_spec(a, l) for a, l in pairs], [a for a, _ in pairs]


def _rms(x, gain, denom=None):
    d = x.shape[-1] if denom is None else denom
    ms = jnp.sum(x * x, axis=-1, keepdims=True) * (1.0 / d)
    return x * lax.rsqrt(ms + EPS) * gain


def _dot(a, b):
    return jnp.dot(a, b, preferred_element_type=F32)


def _dot_nt(a, b):
    return lax.dot_general(a, b, (((1,), (1,)), ((), ())), preferred_element_type=F32)


def _dot_tn(a, b):
    return lax.dot_general(a, b, (((0,), (0,)), ((), ())), preferred_element_type=F32)


def _pipelined_subtiles(rows, matmuls, epilogue):
    sub = min(SUBTILE_ROWS, rows)
    slices = [slice(s * sub, (s + 1) * sub) for s in range(rows // sub)]
    staged = matmuls(slices[0])
    for s, rs in enumerate(slices):
        ready = staged
        if s + 1 < len(slices):
            staged = matmuls(slices[s + 1])
        epilogue(rs, *ready)


def _cast_slabs(cast_next, steps):
    if cast_next is None:
        return [], [], [], []
    weights, layer = cast_next
    in_specs, out_specs, out_shapes = [], [], []
    for w in weights:
        _, r, c = w.shape
        slab = next(s for s in range(CAST_ROW_ALIGN, r + 1, CAST_ROW_ALIGN) if r % s == 0 and r // s <= steps)
        last = r // slab - 1
        in_specs.append(pl.BlockSpec((None, slab, c), lambda i, last=last: (layer, jnp.minimum(i, last), 0)))
        out_specs.append(pl.BlockSpec((slab, c), lambda i, last=last: (jnp.minimum(i, last), 0)))
        out_shapes.append(jax.ShapeDtypeStruct((r, c), BF16))
    return in_specs, list(weights), out_specs, out_shapes


def _cast_side_job(src_refs, dst_refs):
    for src, dst in zip(src_refs, dst_refs):
        dst[...] = src[...].astype(BF16)


def _ffn_body(x_ref, g_ref, wg_ref, wu_ref, wd_ref, *rest):
    n_cast = (len(rest) - 1) // 2
    o_ref = rest[n_cast]

    def matmuls(rs):
        h = _rms(x_ref[rs, :], g_ref[...]).astype(BF16)
        return _dot(h, wg_ref[...]), _dot(h, wu_ref[...])

    def epilogue(rs, gate, up):
        a = (gate * jax.nn.sigmoid(gate) * up).astype(BF16)
        o_ref[rs, :] = x_ref[rs, :] + 0.5 * _dot(a, wd_ref[...])

    _pipelined_subtiles(x_ref.shape[0], matmuls, epilogue)
    _cast_side_job(rest[:n_cast], rest[n_cast + 1:])


def _ffn(x, params, cast_next=None):
    t, d = x.shape
    tm = min(FFN_TILE, t)
    row = pl.BlockSpec((tm, d), lambda i: (i, 0))
    specs, ops = _layered(params)
    c_in, c_ops, c_out, c_shapes = _cast_slabs(cast_next, t // tm)
    out = pl.pallas_call(
        _ffn_body,
        grid=(t // tm,),
        in_specs=[row] + specs + c_in,
        out_specs=[row] + c_out,
        out_shape=[jax.ShapeDtypeStruct((t, d), F32)] + c_shapes,
        compiler_params=_params("arbitrary"),
        name="ffn",
    )(x, *ops, *c_ops)
    return out[0], out[1:]


def _mem_q(p, gain):
    heads = [_rms(p[:, h * MEM_DH:(h + 1) * MEM_DH], gain) * (MEM_DH ** -0.5) for h in range(MEM_HEADS)]
    return jnp.concatenate(heads, axis=-1).astype(BF16)


def _mem_head(qm, km_ref, vm_ref, h):
    hs = slice(h * MEM_DH, (h + 1) * MEM_DH)
    s = _dot_nt(qm[:, hs], km_ref[:, hs])
    p = jnp.exp(s - jnp.max(s, axis=-1, keepdims=True))
    l = jnp.sum(p, axis=-1, keepdims=True)
    return (_dot(p.astype(BF16), vm_ref[:, hs]) / l).astype(BF16)


def _gla_layer_body(x_ref, km_ref, vm_ref, g_ref, w_ref, wa_ref, ba_ref, mqg_ref, gn_ref, wo_ref,
                    out_ref, s_ref, o_s, *, dk, dv):
    c = GLA_CHUNK
    rows = x_ref.shape[0]
    sub = rows
    dqk = GLA_HEADS * dk
    dvt = GLA_HEADS * dv
    dmem = MEM_HEADS * MEM_DH

    @pl.when(pl.program_id(1) == 0)
    def _():
        s_ref[...] = jnp.zeros_like(s_ref)

    ri = lax.broadcasted_iota(jnp.int32, (c, c), 0)
    ci = lax.broadcasted_iota(jnp.int32, (c, c), 1)
    causal = ci <= ri
    tril = jnp.where(causal, 1.0, 0.0).astype(BF16)
    gn = gn_ref[...]

    def projection(rs, res):
        def col(lo, n):
            return _dot(res["h"], w_ref[:, lo:lo + n])

        def s_a():
            res["h"] = _rms(x_ref[rs, :], g_ref[...]).astype(BF16)
            a_low = col(2 * dqk + 2 * dvt + dmem, LANES).astype(BF16)
            z = _dot(a_low, wa_ref[...]) + ba_ref[...]
            log_sig = jnp.minimum(z, 0.0) - jnp.log1p(jnp.exp(-jnp.abs(z)))
            res["la"] = log_sig * (1.0 / GLA_GATE_TAU)

        def s_q():
            res["q"] = col(0, dqk).astype(BF16)

        def s_cum():
            la = res["la"]
            hi = la.astype(BF16)
            rest = la - hi.astype(F32)
            mid = rest.astype(BF16)
            lo = (rest - mid.astype(F32)).astype(BF16)
            res["bc"] = [sum(_dot(tril, t[ic * c:(ic + 1) * c, :]) for t in (hi, mid, lo))
                         for ic in range(sub // c)]

        def s_k():
            res["k"] = col(dqk, dqk).astype(BF16)

        def s_v():
            res["v"] = col(2 * dqk, dvt).astype(BF16)

        def s_r():
            res["r"] = col(2 * dqk + dvt, dvt).astype(BF16)

        def s_m():
            res["qm"] = _mem_q(col(2 * dqk + 2 * dvt, dmem), mqg_ref[...])

        return [s_a, s_q, s_cum, s_k, s_v, s_r, s_m]

    def scan():
        items = [(s, ic, h) for s in range(len(slices)) for ic in range(sub // c) for h in range(GLA_HEADS)]
        ahead = []

        def independent_part(item):
            s, ic, h = item
            res = results[s]
            cs = slice(ic * c, (ic + 1) * c)
            ks = slice(h * dk, (h + 1) * dk)
            bc = res["bc"][ic][:, ks]
            b_last = bc[c - 1:c, :]
            b_mid = bc[c // 2 - 1:c // 2, :]
            q = res["q"][cs, ks].astype(F32) * (dk ** -0.5)
            k = res["k"][cs, ks].astype(F32)
            v = res["v"][cs, h * dv:(h + 1) * dv]
            qe = (q * jnp.exp(bc - b_mid)).astype(BF16)
            ke = (k * jnp.exp(b_mid - bc)).astype(BF16)
            sc = _dot_nt(qe, ke)
            u = _dot_tn((k * jnp.exp(b_last - bc)).astype(BF16), v)
            qi = (q * jnp.exp(bc)).astype(BF16)
            dcol = jnp.exp(jnp.broadcast_to(b_last, (dk, dk))).T
            ahead.append((sc, u, qi, v, dcol))

        def dependent_part(item):
            s, ic, h = item
            res, r0 = results[s], s * sub
            sc, u, qi, v, dcol = ahead.pop(0)
            cs = slice(ic * c, (ic + 1) * c)
            vs = slice(h * dv, (h + 1) * dv)
            s_prev = s_ref[h]
            o = _dot(jnp.concatenate([qi, jnp.where(causal, sc, 0.0).astype(BF16)], axis=1),
                     jnp.concatenate([s_prev.astype(BF16), v], axis=0))
            s_ref[h] = jnp.concatenate([dcol] * (dv // dk), axis=1) * s_prev + u
            r = res["r"][cs, vs].astype(F32)
            o_s[r0 + ic * c:r0 + (ic + 1) * c, vs] = (_rms(o, gn) * (r * jax.nn.sigmoid(r))).astype(BF16)

        steps = [functools.partial(independent_part, it) for it in items[:GLA_LOOKAHEAD]]
        for i, it in enumerate(items):
            if i + GLA_LOOKAHEAD < len(items):
                steps.append(functools.partial(independent_part, items[i + GLA_LOOKAHEAD]))
            steps.append(functools.partial(dependent_part, it))
        return steps

    def output(rs, res):
        heads = []
        acc = {}

        def mem_head(h):
            heads.append(_mem_head(res["qm"], km_ref, vm_ref, h))

        def mix():
            acc["y"] = _dot(o_s[rs, :], wo_ref[:dvt, :])

        def finish():
            y = acc["y"] + _dot(jnp.concatenate(heads, axis=-1), wo_ref[dvt:, :])
            out_ref[rs, :] = x_ref[rs, :] + y

        return [functools.partial(mem_head, h) for h in range(MEM_HEADS)] + [mix, finish]

    slices = [slice(0, rows)]
    results = [{}]
    for step in projection(slices[0], results[0]) + scan() + output(slices[0], results[0]):
        step()


def _gla_layer(x, k_mem, v_mem, params, batch, seq, dk, dv):
    t, d = x.shape
    tm = min(TOKEN_TILE, seq)
    nb = seq // tm
    row = pl.BlockSpec((tm, d), lambda b, i: (b * nb + i, 0))
    mem_blk = pl.BlockSpec((None,) + k_mem.shape[1:], lambda b, i: (b, 0, 0))
    specs, ops = _layered(params)
    return pl.pallas_call(
        functools.partial(_gla_layer_body, dk=dk, dv=dv),
        grid=(batch, nb),
        in_specs=[row, mem_blk, mem_blk] + specs,
        out_specs=row,
        out_shape=jax.ShapeDtypeStruct((t, d), F32),
        scratch_shapes=[pltpu.VMEM((GLA_HEADS, dk, dv), F32), pltpu.VMEM((tm, GLA_HEADS * dv), BF16)],
        compiler_params=_params("parallel", "arbitrary"),
        name="gla_layer",
    )(x, k_mem, v_mem, *ops)


def _rope(x, cos, sin):
    return x * cos + pltpu.roll(x, LANES // 2, axis=1) * sin


def _ffn_mla_in_body(x_ref, fg_ref, wg_ref, wu_ref, wd_ref,
                     g_ref, w_ref, qn_ref, kvn_ref, wuq_ref, wuk_ref, wuvt_ref, qg_ref, kg_ref, krg_ref, mqg_ref,
                     cos_ref, sin_ref, *rest):
    n_cast = (len(rest) - 8) // 2
    x_out, q_out, k_out, vt_out, qm_out = rest[n_cast:n_cast + 5]
    pt_s, q_s, kn_s = rest[-3:]
    o1 = MLA_Q_RANK
    o2 = o1 + MLA_KV_RANK
    qg = qg_ref[...]

    def ffn_in(rs):
        h = _rms(x_ref[rs, :], fg_ref[...]).astype(BF16)
        return _dot(h, wg_ref[...]), _dot(h, wu_ref[...])

    def ffn_out(rs, gate, up):
        a = (gate * jax.nn.sigmoid(gate) * up).astype(BF16)
        xn = x_ref[rs, :] + 0.5 * _dot(a, wd_ref[...])
        x_out[rs, :] = xn
        return xn

    def matmuls(xn):
        h = _rms(xn, g_ref[...]).astype(BF16)
        p = _dot(h, w_ref[...])
        cq = _rms(p[:, :o1], qn_ref[...]).astype(BF16)
        ckv = _rms(p[:, o1:o2], kvn_ref[...]).astype(BF16)
        return p[:, o2:], _dot(cq, wuq_ref[...]), _dot(ckv, wuk_ref[...]), _dot_nt(wuvt_ref[...], ckv)

    def epilogue(rs, p_tail, q, kn_all):
        qm_out[rs, :] = _mem_q(p_tail[:, LANES:], mqg_ref[...])
        cos = cos_ref[rs, :]
        sin = sin_ref[rs, :]
        k_rope = _rope(_rms(p_tail[:, :LANES], krg_ref[...], denom=MLA_ROPE), cos, sin).astype(BF16)
        for hd in range(MLA_HEADS):
            b0 = hd * MLA_QK_PAD
            qn = _rms(q[:, b0:b0 + LANES], qg[:, :LANES])
            qr = _rope(_rms(q[:, b0 + LANES:b0 + MLA_QK_PAD], qg[:, LANES:], denom=MLA_ROPE), cos, sin)
            q_out[rs, b0:b0 + LANES] = qn.astype(BF16)
            q_out[rs, b0 + LANES:b0 + MLA_QK_PAD] = qr.astype(BF16)
            kn = _rms(kn_all[:, hd * MLA_NOPE:(hd + 1) * MLA_NOPE], kg_ref[...])
            k_out[rs, b0:b0 + LANES] = kn.astype(BF16)
            k_out[rs, b0 + LANES:b0 + MLA_QK_PAD] = k_rope

    rows = x_ref.shape[0]
    sub = min(SUBTILE_ROWS, rows)
    slices = [slice(s * sub, (s + 1) * sub) for s in range(rows // sub)]

    @pl.when(pl.program_id(0) == 0)
    def _():
        pt_s[...] = jnp.zeros_like(pt_s)
        q_s[...] = jnp.zeros_like(q_s)
        kn_s[...] = jnp.zeros_like(kn_s)

    hidden = [ffn_in(slices[0])]
    for rs in slices:
        epilogue(rs, pt_s[rs, :], q_s[rs, :], kn_s[rs, :])
    hidden += [ffn_in(rs) for rs in slices[1:]]
    for rs, gate_up in zip(slices, hidden):
        p_tail, q, kn_all, vt = matmuls(ffn_out(rs, *gate_up))
        vt_out[:, rs] = vt.astype(BF16)
        pt_s[rs, :] = p_tail
        q_s[rs, :] = q
        kn_s[rs, :] = kn_all
    _cast_side_job(rest[:n_cast], rest[n_cast + 5:-3])


def _ffn_mla_in(x, params, cos, sin, seq, cast_next=None):
    t, d = x.shape
    tm = min(TOKEN_TILE, t, seq)
    nb = seq // tm
    last = t // tm - 1
    dq = MLA_HEADS * MLA_QK_PAD
    dvt = MLA_HEADS * MLA_DV
    dmem = MEM_HEADS * MEM_DH
    cur = lambda n: pl.BlockSpec((tm, n), lambda i: (jnp.minimum(i, last), 0))
    prev = lambda n: pl.BlockSpec((tm, n), lambda i: (jnp.maximum(i - 1, 0), 0))
    pos = pl.BlockSpec((tm, LANES), lambda i: (jnp.maximum(i - 1, 0) % nb, 0))
    specs, ops = _layered(params)
    p_tail = LANES + dmem
    c_in, c_ops, c_out, c_shapes = _cast_slabs(cast_next, t // tm + 1)
    out = pl.pallas_call(
        _ffn_mla_in_body,
        grid=(t // tm + 1,),
        in_specs=[cur(d)] + specs + [pos, pos] + c_in,
        out_specs=[cur(d), prev(dq), prev(dq), pl.BlockSpec((dvt, tm), lambda i: (0, jnp.minimum(i, last))),
                   prev(dmem)] + c_out,
        out_shape=[jax.ShapeDtypeStruct((t, d), F32), jax.ShapeDtypeStruct((t, dq), BF16),
                   jax.ShapeDtypeStruct((t, dq), BF16), jax.ShapeDtypeStruct((dvt, t), BF16),
                   jax.ShapeDtypeStruct((t, dmem), BF16)] + c_shapes,
        scratch_shapes=[pltpu.VMEM((tm, p_tail), F32), pltpu.VMEM((tm, dq), F32),
                        pltpu.VMEM((tm, MLA_HEADS * MLA_NOPE), F32)],
        compiler_params=_params("arbitrary"),
        name="ffn_mla_in",
    )(x, *ops, cos, sin, *c_ops)
    return out[:5], out[5:]


def _flash_body(q_ref, k_ref, vt_ref, o_ref, *, tq, tk, rc):
    qi = pl.program_id(2)
    n_diag = tq // tk
    n_chain = tq // rc
    dv = vt_ref.shape[0]
    ones = jnp.ones((ONES_ROWS, tk), BF16)

    def run_blocks(j0, state, diagonal):
        items = []
        for d in range(n_diag):
            for c in range(n_chain):
                lo_q, hi_q = c * rc, (c + 1) * rc - 1
                lo_k, hi_k = d * tk, (d + 1) * tk - 1
                if diagonal and lo_k > hi_q:
                    continue
                masked = diagonal and hi_k > lo_q
                assert not masked or lo_k <= lo_q
                items.append((d, c, masked))
        blocks = {}

        def key_block(d):
            if d not in blocks:
                k0 = pl.multiple_of((j0 + d) * tk, tk)
                blocks[d] = (k_ref[pl.ds(k0, tk), :],
                             jnp.concatenate([vt_ref[:, pl.ds(k0, tk)], ones], axis=0))
            return blocks[d]

        def scores(item):
            d, c, _ = item
            return _dot_nt(key_block(d)[0], q_ref[c * rc:(c + 1) * rc, :])

        state = list(state)
        ahead = [scores(it) for it in items[:FLASH_LOOKAHEAD]]
        for i, (d, c, masked) in enumerate(items):
            if i + FLASH_LOOKAHEAD < len(items):
                ahead.append(scores(items[i + FLASH_LOOKAHEAD]))
            st = ahead.pop(0)
            if masked:
                kpos = d * tk + lax.broadcasted_iota(jnp.int32, (tk, rc), 0)
                qpos = c * rc + lax.broadcasted_iota(jnp.int32, (tk, rc), 1)
                st = jnp.where(kpos <= qpos, st, NEG_BIG)
            m, acc = state[c]
            m_new = jnp.maximum(m, jnp.max(st, axis=0, keepdims=True))
            alpha = jnp.exp2(m - m_new)
            pt = jnp.exp2(st - m_new).astype(BF16)
            state[c] = (m_new, alpha * acc + _dot(key_block(d)[1], pt))
        return tuple(state)

    init = tuple((jnp.full((1, rc), NEG_BIG, F32), jnp.zeros((dv + ONES_ROWS, rc), F32)) for _ in range(n_chain))
    state = lax.fori_loop(0, qi, lambda t, st: run_blocks(t * n_diag, st, False), init)
    state = run_blocks(qi * n_diag, state, True)
    for c in range(n_chain):
        acc = state[c][1]
        o_ref[c * rc:(c + 1) * rc, :] = (acc[:dv, :] / acc[dv:dv + 1, :]).T.astype(o_ref.dtype)


def _flash(q, k, vt, batch, seq):
    tq = min(FLASH_TQ, seq)
    tk = min(FLASH_TK, tq)
    rc = min(FLASH_CHAIN_ROWS, tq)
    q3 = q.reshape(batch, seq, -1)
    k3 = k.reshape(batch, seq, -1)
    out = pl.pallas_call(
        functools.partial(_flash_body, tq=tq, tk=tk, rc=rc),
        grid=(batch, MLA_HEADS, seq // tq),
        in_specs=[pl.BlockSpec((None, tq, MLA_QK_PAD), lambda b, h, i: (b, i, h)),
                  pl.BlockSpec((None, seq, MLA_QK_PAD), lambda b, h, i: (b, 0, h)),
                  pl.BlockSpec((MLA_DV, seq), lambda b, h, i: (h, b))],
        out_specs=pl.BlockSpec((None, tq, MLA_DV), lambda b, h, i: (b, i, h)),
        out_shape=jax.ShapeDtypeStruct((batch, seq, MLA_HEADS * MLA_DV), BF16),
        compiler_params=_params("parallel", "parallel", "arbitrary"),
        name="mla_flash",
    )(q3, k3, vt)
    return out.reshape(batch * seq, -1)


def _memkv_body(mem_ref, g_ref, w_ref, kg_ref, k_out, v_out):
    m = _rms(mem_ref[...], g_ref[...]).astype(BF16)
    kv = _dot(m, w_ref[...])
    dm = MEM_HEADS * MEM_DH
    heads = [_rms(kv[:, h * MEM_DH:(h + 1) * MEM_DH], kg_ref[...]) for h in range(MEM_HEADS)]
    k_out[...] = jnp.concatenate(heads, axis=-1).astype(BF16)
    v_out[...] = kv[:, dm:].astype(BF16)


def _memkv(mem, params):
    b, m, d = mem.shape
    specs, ops = _layered(params)
    dm = MEM_HEADS * MEM_DH
    blk = pl.BlockSpec((None, m, dm), lambda i: (i, 0, 0))
    return pl.pallas_call(
        _memkv_body,
        grid=(b,),
        in_specs=[pl.BlockSpec((None, m, d), lambda i: (i, 0, 0))] + specs,
        out_specs=[blk, blk],
        out_shape=[jax.ShapeDtypeStruct((b, m, dm), BF16)] * 2,
        compiler_params=_params("parallel"),
        name="mem_kv",
    )(mem, *ops)


def _out_body(x_ref, o_ref, qm_ref, km_ref, vm_ref, wo_ref, out_ref):
    qm = qm_ref[...]
    om = jnp.concatenate([_mem_head(qm, km_ref, vm_ref, h) for h in range(MEM_HEADS)], axis=-1)
    dmix = o_ref.shape[-1]
    y = _dot(o_ref[...], wo_ref[:dmix, :]) + _dot(om, wo_ref[dmix:, :])
    out_ref[...] = x_ref[...] + y


def _out_proj(x, o, q_mem, k_mem, v_mem, w_out, seq):
    t, d = x.shape
    tm = min(TOKEN_TILE, t, seq)
    nb = seq // tm
    row = lambda n: pl.BlockSpec((tm, n), lambda i: (i, 0))
    mem_blk = pl.BlockSpec((None,) + k_mem.shape[1:], lambda i: (i // nb, 0, 0))
    return pl.pallas_call(
        _out_body,
        grid=(t // tm,),
        in_specs=[row(d), row(o.shape[1]), row(q_mem.shape[1]), mem_blk, mem_blk, _layer_spec(*w_out)],
        out_specs=row(d),
        out_shape=jax.ShapeDtypeStruct((t, d), F32),
        compiler_params=_params("parallel"),
        name="out_proj",
    )(x, o, q_mem, k_mem, v_mem, w_out[0])


def _pad_rope(w):
    half = MLA_ROPE // 2
    z = jnp.zeros(w.shape[:-1] + (LANES // 2 - half,), w.dtype)
    return jnp.concatenate([w[..., :half], z, w[..., half:], z], axis=-1)


def _rope_tables(seq):
    inv = 1.0 / (ROPE_BASE ** (jnp.arange(0, MLA_ROPE, 2, dtype=F32) / MLA_ROPE))
    ang = jnp.arange(seq, dtype=F32)[:, None] * inv[None, :]
    cos = _pad_rope(jnp.concatenate([jnp.cos(ang)] * 2, axis=-1))
    sin = _pad_rope(jnp.concatenate([-jnp.sin(ang), jnp.sin(ang)], axis=-1))
    return cos, sin


def kernel(x, mem, ffn1_norm, ffn1_w_gate, ffn1_w_up, ffn1_w_down, ffn2_norm, ffn2_w_gate, ffn2_w_up, ffn2_w_down, mix_norm, w_out, mem_norm, mem_w_kv, memq_norm, memk_norm, gla_w_in, gla_w_alpha, gla_b_alpha, gla_out_norm, mla_w_in, mla_q_norm, mla_kv_norm, mla_w_uq, mla_w_ukv, mla_qn_norm, mla_qr_norm, mla_kn_norm, mla_kr_norm):
    batch, seq, d = x.shape
    depth = ffn1_norm.shape[0]
    t = batch * seq
    xs = x.reshape(t, d)
    cos, sin = _rope_tables(seq)
    gains = lambda a: a.astype(F32)[:, None, :]
    bf = lambda a: a.astype(BF16)

    ffn_g = [gains(ffn1_norm), gains(ffn2_norm)]
    ffn_w = [[ffn1_w_gate, ffn1_w_up, ffn1_w_down], [ffn2_w_gate, ffn2_w_up, ffn2_w_down]]
    ffn_calls = [(half, i) for i in range(depth) for half in (0, 1)]
    ffn_bf = [bf(w[:1]) for w in ffn_w[0]]
    mix_g, memq_g = gains(mix_norm), gains(memq_norm)
    side = [w.reshape(1, -1, w.shape[-1]) for w in (w_out, mem_w_kv)]

    gla_dqk = gla_w_alpha.shape[-1]
    gla_dv = d
    a_lo = 2 * gla_dqk + gla_dv
    a_hi = a_lo + GLA_GATE_RANK
    pad_rank = LANES - GLA_GATE_RANK
    gla_w = bf(jnp.concatenate([gla_w_in[..., :a_lo], gla_w_in[..., a_hi:],
                                jnp.pad(gla_w_in[..., a_lo:a_hi], ((0, 0), (0, 0), (0, pad_rank)))], axis=-1))
    gla_wa = bf(jnp.pad(gla_w_alpha, ((0, 0), (0, pad_rank), (0, 0))))
    gla_ba, gla_og = gains(gla_b_alpha), gains(gla_out_norm)

    r_lo = MLA_Q_RANK + MLA_KV_RANK
    r_hi = r_lo + MLA_ROPE
    n_mla = mla_w_in.shape[0]
    mla_w = bf(jnp.concatenate([mla_w_in[..., :r_lo], _pad_rope(mla_w_in[..., r_lo:r_hi]),
                                mla_w_in[..., r_hi:]], axis=-1))
    uq = mla_w_uq.reshape(n_mla, MLA_Q_RANK, MLA_HEADS, MLA_NOPE + MLA_ROPE)
    w_uq = bf(jnp.concatenate([uq[..., :MLA_NOPE], _pad_rope(uq[..., MLA_NOPE:])], axis=-1)
              .reshape(n_mla, MLA_Q_RANK, MLA_HEADS * MLA_QK_PAD))
    ukv = mla_w_ukv.reshape(n_mla, MLA_KV_RANK, MLA_HEADS, MLA_NOPE + MLA_DV)
    w_uk = bf(ukv[..., :MLA_NOPE].reshape(n_mla, MLA_KV_RANK, -1))
    w_uvt = bf(jnp.swapaxes(ukv[..., MLA_NOPE:].reshape(n_mla, MLA_KV_RANK, -1), 1, 2))
    qk_scale = (MLA_NOPE + MLA_ROPE) ** -0.5 * LOG2_E
    mla_qg = gains(jnp.concatenate([mla_qn_norm, _pad_rope(mla_qr_norm)], axis=-1) * qk_scale)
    mla_qn, mla_kvn = gains(mla_q_norm), gains(mla_kv_norm)
    mla_kg, mla_krg = gains(mla_kn_norm), gains(_pad_rope(mla_kr_norm))

    def ffn_args(half, i):
        n = ffn_calls.index((half, i)) + 1
        nxt = (ffn_w[ffn_calls[n][0]], ffn_calls[n][1]) if n < len(ffn_calls) else None
        return [(ffn_g[half], i)] + [(w[None] if w.ndim == 2 else w, 0) for w in ffn_bf], nxt

    for i in range(depth):
        j = i // 2
        params, nxt = ffn_args(0, i)
        if i == 0:
            xs, cast = _ffn(xs, params, (nxt[0] + side, 0))
            ffn_bf = cast[:len(nxt[0])]
            w_out_b, mem_w_b = [c.reshape(w.shape) for c, w in zip(cast[len(nxt[0]):], (w_out, mem_w_kv))]
            memkv = [gains(mem_norm), mem_w_b, gains(memk_norm)]
        elif i % 2 == 0:
            xs, ffn_bf = _ffn(xs, params, nxt)
        k_mem, v_mem = _memkv(mem, [(a, i) for a in memkv])
        if i % 2 == 0:
            xs = _gla_layer(xs, k_mem, v_mem, [(mix_g, i), (gla_w, j), (gla_wa, j), (gla_ba, j), (memq_g, i),
                                               (gla_og, j), (w_out_b, i)],
                            batch, seq, gla_dqk // GLA_HEADS, gla_dv // GLA_HEADS)
        else:
            (xs, q, k, vt, q_mem), ffn_bf = _ffn_mla_in(
                xs, params + [(mix_g, i), (mla_w, j), (mla_qn, j), (mla_kvn, j), (w_uq, j), (w_uk, j),
                              (w_uvt, j), (mla_qg, j), (mla_kg, j), (mla_krg, j), (memq_g, i)],
                cos, sin, seq, nxt)
            o = _flash(q, k, vt, batch, seq)
            xs = _out_proj(xs, o, q_mem, k_mem, v_mem, (w_out_b, i), seq)
        params, nxt = ffn_args(1, i)
        xs, ffn_bf = _ffn(xs, params, nxt)
    return xs.reshape(batch, seq, d)
```

```python
import functools

import jax
import jax.numpy as jnp
from jax import lax
from jax.experimental import pallas as pl
from jax.experimental.pallas import tpu as pltpu

F32 = jnp.float32
BF16 = jnp.bfloat16

EPS = 1e-6
GLA_HEADS = 4
GLA_GATE_RANK = 16
GLA_GATE_TAU = 16.0
GLA_CHUNK = 64
MLA_HEADS = 8
MLA_Q_RANK = 384
MLA_KV_RANK = 256
MLA_NOPE = 128
MLA_ROPE = 64
MLA_DV = 128
ROPE_BASE = 10000.0
MEM_HEADS = 4
MEM_DH = 128

LANES = 128
MLA_QK_PAD = 2 * LANES
VMEM_LIMIT_BYTES = 56 * 1024 * 1024

TOKEN_TILE = 512
WIDE_TILE = 1024
SUBTILE_ROWS = 256
FLASH_TQ = 2048
FLASH_TK = 256
FLASH_CHAIN_ROWS = 256
FLASH_LOOKAHEAD = 16
CAST_ROW_ALIGN = 16
ONES_ROWS = 16
GLA_LOOKAHEAD = 4

NEG_BIG = -1e30
LOG2_E = 1.4426950408889634


def _params(*sem):
    return pltpu.CompilerParams(dimension_semantics=sem, vmem_limit_bytes=VMEM_LIMIT_BYTES)


def _layer_spec(stacked, layer):
    nd = stacked.ndim - 1
    return pl.BlockSpec((None,) + stacked.shape[1:], lambda *_: (layer,) + (0,) * nd,
                        pipeline_mode=pl.Buffered(1))


def _layered(pairs):
    return [_layer_spec(a, l) for a, l in pairs], [a for a, _ in pairs]


def _rms(x, gain, denom=None):
    d = x.shape[-1] if denom is None else denom
    ms = jnp.sum(x * x, axis=-1, keepdims=True) * (1.0 / d)
    return x * lax.rsqrt(ms + EPS) * gain


def _dot(a, b):
    return jnp.dot(a, b, preferred_element_type=F32)


def _dot_nt(a, b):
    return lax.dot_general(a, b, (((1,), (1,)), ((), ())), preferred_element_type=F32)


def _dot_tn(a, b):
    return lax.dot_general(a, b, (((0,), (0,)), ((), ())), preferred_element_type=F32)


def _pipelined_subtiles(rows, matmuls, epilogue):
    sub = min(SUBTILE_ROWS, rows)
    slices = [slice(s * sub, (s + 1) * sub) for s in range(rows // sub)]
    staged = matmuls(slices[0])
    for s, rs in enumerate(slices):
        ready = staged
        if s + 1 < len(slices):
            staged = matmuls(slices[s + 1])
        epilogue(rs, *ready)


def _cast_slabs(cast_next, steps):
    if cast_next is None:
        return [], [], [], []
    weights, layer = cast_next
    in_specs, out_specs, out_shapes = [], [], []
    for w in weights:
        _, r, c = w.shape
        slab = next(s for s in range(CAST_ROW_ALIGN, r + 1, CAST_ROW_ALIGN) if r % s == 0 and r // s <= steps)
        last = r // slab - 1
        in_specs.append(pl.BlockSpec((None, slab, c), lambda i, last=last: (layer, jnp.minimum(i, last), 0)))
        out_specs.append(pl.BlockSpec((slab, c), lambda i, last=last: (jnp.minimum(i, last), 0)))
        out_shapes.append(jax.ShapeDtypeStruct((r, c), BF16))
    return in_specs, list(weights), out_specs, out_shapes


def _cast_side_job(src_refs, dst_refs):
    for src, dst in zip(src_refs, dst_refs):
        dst[...] = src[...].astype(BF16)


def _ffn_body(x_ref, g_ref, wg_ref, wu_ref, wd_ref, *rest):
    n_cast = (len(rest) - 1) // 2
    o_ref = rest[n_cast]

    def matmuls(rs):
        h = _rms(x_ref[rs, :], g_ref[...]).astype(BF16)
        return _dot(h, wg_ref[...]), _dot(h, wu_ref[...])

    def epilogue(rs, gate, up):
        a = (gate * jax.nn.sigmoid(gate) * up).astype(BF16)
        o_ref[rs, :] = x_ref[rs, :] + 0.5 * _dot(a, wd_ref[...])

    _pipelined_subtiles(x_ref.shape[0], matmuls, epilogue)
    _cast_side_job(rest[:n_cast], rest[n_cast + 1:])


def _ffn(x, params, cast_next=None):
    t, d = x.shape
    tm = min(WIDE_TILE, t)
    row = pl.BlockSpec((tm, d), lambda i: (i, 0))
    specs, ops = _layered(params)
    c_in, c_ops, c_out, c_shapes = _cast_slabs(cast_next, t // tm)
    out = pl.pallas_call(
        _ffn_body,
        grid=(t // tm,),
        in_specs=[row] + specs + c_in,
        out_specs=[row] + c_out,
        out_shape=[jax.ShapeDtypeStruct((t, d), F32)] + c_shapes,
        compiler_params=_params("arbitrary"),
        name="ffn",
    )(x, *ops, *c_ops)
    return out[0], out[1:]


def _mem_q(p, gain):
    heads = [_rms(p[:, h * MEM_DH:(h + 1) * MEM_DH], gain) * (MEM_DH ** -0.5) for h in range(MEM_HEADS)]
    return jnp.concatenate(heads, axis=-1).astype(BF16)


def _mem_head(qm, km_ref, vm_ref, h):
    hs = slice(h * MEM_DH, (h + 1) * MEM_DH)
    s = _dot_nt(qm[:, hs], km_ref[:, hs])
    p = jnp.exp(s - jnp.max(s, axis=-1, keepdims=True))
    l = jnp.sum(p, axis=-1, keepdims=True)
    return (_dot(p.astype(BF16), vm_ref[:, hs]) / l).astype(BF16)


def _gla_layer_body(x_ref, km_ref, vm_ref, g_ref, w_ref, wa_ref, ba_ref, mqg_ref, gn_ref, wo_ref,
                    out_ref, s_ref, o_s, *, dk, dv):
    c = GLA_CHUNK
    rows = x_ref.shape[0]
    sub = rows
    dqk = GLA_HEADS * dk
    dvt = GLA_HEADS * dv
    dmem = MEM_HEADS * MEM_DH

    @pl.when(pl.program_id(1) == 0)
    def _():
        s_ref[...] = jnp.zeros_like(s_ref)

    ri = lax.broadcasted_iota(jnp.int32, (c, c), 0)
    ci = lax.broadcasted_iota(jnp.int32, (c, c), 1)
    causal = ci <= ri
    tril = jnp.where(causal, 1.0, 0.0).astype(BF16)
    gn = gn_ref[...]

    def projection(rs, res):
        def col(lo, n):
            return _dot(res["h"], w_ref[:, lo:lo + n])

        def s_a():
            res["h"] = _rms(x_ref[rs, :], g_ref[...]).astype(BF16)
            a_low = col(2 * dqk + 2 * dvt + dmem, LANES).astype(BF16)
            z = _dot(a_low, wa_ref[...]) + ba_ref[...]
            log_sig = jnp.minimum(z, 0.0) - jnp.log1p(jnp.exp(-jnp.abs(z)))
            res["la"] = log_sig * (1.0 / GLA_GATE_TAU)

        def s_q():
            res["q"] = col(0, dqk).astype(BF16)

        def s_cum():
            la = res["la"]
            hi = la.astype(BF16)
            rest = la - hi.astype(F32)
            mid = rest.astype(BF16)
            lo = (rest - mid.astype(F32)).astype(BF16)
            res["bc"] = [sum(_dot(tril, t[ic * c:(ic + 1) * c, :]) for t in (hi, mid, lo))
                         for ic in range(sub // c)]

        def s_k():
            res["k"] = col(dqk, dqk).astype(BF16)

        def s_v():
            res["v"] = col(2 * dqk, dvt).astype(BF16)

        def s_r():
            res["r"] = col(2 * dqk + dvt, dvt).astype(BF16)

        def s_m():
            res["qm"] = _mem_q(col(2 * dqk + 2 * dvt, dmem), mqg_ref[...])

        return [s_a, s_q, s_cum, s_k, s_v, s_r, s_m]

    def scan():
        items = [(s, ic, h) for s in range(len(slices)) for ic in range(sub // c) for h in range(GLA_HEADS)]
        ahead = []

        def independent_part(item):
            s, ic, h = item
            res = results[s]
            cs = slice(ic * c, (ic + 1) * c)
            ks = slice(h * dk, (h + 1) * dk)
            bc = res["bc"][ic][:, ks]
            b_last = bc[c - 1:c, :]
            b_mid = bc[c // 2 - 1:c // 2, :]
            q = res["q"][cs, ks].astype(F32) * (dk ** -0.5)
            k = res["k"][cs, ks].astype(F32)
            v = res["v"][cs, h * dv:(h + 1) * dv]
            qe = (q * jnp.exp(bc - b_mid)).astype(BF16)
            ke = (k * jnp.exp(b_mid - bc)).astype(BF16)
            sc = _dot_nt(qe, ke)
            u = _dot_tn((k * jnp.exp(b_last - bc)).astype(BF16), v)
            qi = (q * jnp.exp(bc)).astype(BF16)
            dcol = jnp.exp(jnp.broadcast_to(b_last, (dk, dk))).T
            ahead.append((sc, u, qi, v, dcol))

        def dependent_part(item):
            s, ic, h = item
            res, r0 = results[s], s * sub
            sc, u, qi, v, dcol = ahead.pop(0)
            cs = slice(ic * c, (ic + 1) * c)
            vs = slice(h * dv, (h + 1) * dv)
            s_prev = s_ref[h]
            o = _dot(jnp.concatenate([qi, jnp.where(causal, sc, 0.0).astype(BF16)], axis=1),
                     jnp.concatenate([s_prev.astype(BF16), v], axis=0))
            s_ref[h] = jnp.concatenate([dcol] * (dv // dk), axis=1) * s_prev + u
            r = res["r"][cs, vs].astype(F32)
            o_s[r0 + ic * c:r0 + (ic + 1) * c, vs] = (_rms(o, gn) * (r * jax.nn.sigmoid(r))).astype(BF16)

        steps = [functools.partial(independent_part, it) for it in items[:GLA_LOOKAHEAD]]
        for i, it in enumerate(items):
            if i + GLA_LOOKAHEAD < len(items):
                steps.append(functools.partial(independent_part, items[i + GLA_LOOKAHEAD]))
            steps.append(functools.partial(dependent_part, it))
        return steps

    def output(rs, res):
        heads = []
        acc = {}

        def mem_head(h):
            heads.append(_mem_head(res["qm"], km_ref, vm_ref, h))

        def mix():
            acc["y"] = _dot(o_s[rs, :], wo_ref[:dvt, :])

        def finish():
            y = acc["y"] + _dot(jnp.concatenate(heads, axis=-1), wo_ref[dvt:, :])
            out_ref[rs, :] = x_ref[rs, :] + y

        return [functools.partial(mem_head, h) for h in range(MEM_HEADS)] + [mix, finish]

    slices = [slice(0, rows)]
    results = [{}]
    for step in projection(slices[0], results[0]) + scan() + output(slices[0], results[0]):
        step()


def _gla_layer(x, k_mem, v_mem, params, batch, seq, dk, dv):
    t, d = x.shape
    tm = min(WIDE_TILE, seq)
    nb = seq // tm
    row = pl.BlockSpec((tm, d), lambda b, i: (b * nb + i, 0))
    mem_blk = pl.BlockSpec((None,) + k_mem.shape[1:], lambda b, i: (b, 0, 0))
    specs, ops = _layered(params)
    return pl.pallas_call(
        functools.partial(_gla_layer_body, dk=dk, dv=dv),
        grid=(batch, nb),
        in_specs=[row, mem_blk, mem_blk] + specs,
        out_specs=row,
        out_shape=jax.ShapeDtypeStruct((t, d), F32),
        scratch_shapes=[pltpu.VMEM((GLA_HEADS, dk, dv), F32), pltpu.VMEM((tm, GLA_HEADS * dv), BF16)],
        compiler_params=_params("parallel", "arbitrary"),
        name="gla_layer",
    )(x, k_mem, v_mem, *ops)


def _rope(x, cos, sin):
    return x * cos + pltpu.roll(x, LANES // 2, axis=1) * sin


def _ffn_mla_in_body(x_ref, fg_ref, wg_ref, wu_ref, wd_ref,
                     g_ref, w_ref, qn_ref, kvn_ref, wuq_ref, wuk_ref, wuvt_ref, qg_ref, kg_ref, krg_ref, mqg_ref,
                     cos_ref, sin_ref, *rest):
    n_cast = (len(rest) - 8) // 2
    x_out, q_out, k_out, vt_out, qm_out = rest[n_cast:n_cast + 5]
    pt_s, q_s, kn_s = rest[-3:]
    o1 = MLA_Q_RANK
    o2 = o1 + MLA_KV_RANK
    qg = qg_ref[...]

    def ffn_in(rs):
        h = _rms(x_ref[rs, :], fg_ref[...]).astype(BF16)
        return _dot(h, wg_ref[...]), _dot(h, wu_ref[...])

    def ffn_out(rs, gate, up):
        a = (gate * jax.nn.sigmoid(gate) * up).astype(BF16)
        xn = x_ref[rs, :] + 0.5 * _dot(a, wd_ref[...])
        x_out[rs, :] = xn
        return xn

    def matmuls(xn):
        h = _rms(xn, g_ref[...]).astype(BF16)
        p = _dot(h, w_ref[...])
        cq = _rms(p[:, :o1], qn_ref[...]).astype(BF16)
        ckv = _rms(p[:, o1:o2], kvn_ref[...]).astype(BF16)
        return p[:, o2:], _dot(cq, wuq_ref[...]), _dot(ckv, wuk_ref[...]), _dot_nt(wuvt_ref[...], ckv)

    def epilogue(rs, p_tail, q, kn_all):
        qm_out[rs, :] = _mem_q(p_tail[:, LANES:], mqg_ref[...])
        cos = cos_ref[rs, :]
        sin = sin_ref[rs, :]
        k_rope = _rope(_rms(p_tail[:, :LANES], krg_ref[...], denom=MLA_ROPE), cos, sin).astype(BF16)
        for hd in range(MLA_HEADS):
            b0 = hd * MLA_QK_PAD
            qn = _rms(q[:, b0:b0 + LANES], qg[:, :LANES])
            qr = _rope(_rms(q[:, b0 + LANES:b0 + MLA_QK_PAD], qg[:, LANES:], denom=MLA_ROPE), cos, sin)
            q_out[rs, b0:b0 + LANES] = qn.astype(BF16)
            q_out[rs, b0 + LANES:b0 + MLA_QK_PAD] = qr.astype(BF16)
            kn = _rms(kn_all[:, hd * MLA_NOPE:(hd + 1) * MLA_NOPE], kg_ref[...])
            k_out[rs, b0:b0 + LANES] = kn.astype(BF16)
            k_out[rs, b0 + LANES:b0 + MLA_QK_PAD] = k_rope

    rows = x_ref.shape[0]
    sub = min(SUBTILE_ROWS, rows)
    slices = [slice(s * sub, (s + 1) * sub) for s in range(rows // sub)]

    @pl.when(pl.program_id(0) == 0)
    def _():
        pt_s[...] = jnp.zeros_like(pt_s)
        q_s[...] = jnp.zeros_like(q_s)
        kn_s[...] = jnp.zeros_like(kn_s)

    hidden = [ffn_in(slices[0])]
    for rs in slices:
        epilogue(rs, pt_s[rs, :], q_s[rs, :], kn_s[rs, :])
    hidden += [ffn_in(rs) for rs in slices[1:]]
    for rs, gate_up in zip(slices, hidden):
        p_tail, q, kn_all, vt = matmuls(ffn_out(rs, *gate_up))
        vt_out[:, rs] = vt.astype(BF16)
        pt_s[rs, :] = p_tail
        q_s[rs, :] = q
        kn_s[rs, :] = kn_all
    _cast_side_job(rest[:n_cast], rest[n_cast + 5:-3])


def _ffn_mla_in(x, params, cos, sin, seq, cast_next=None):
    t, d = x.shape
    tm = min(TOKEN_TILE, t, seq)
    nb = seq // tm
    last = t // tm - 1
    dq = MLA_HEADS * MLA_QK_PAD
    dvt = MLA_HEADS * MLA_DV
    dmem = MEM_HEADS * MEM_DH
    cur = lambda n: pl.BlockSpec((tm, n), lambda i: (jnp.minimum(i, last), 0))
    prev = lambda n: pl.BlockSpec((tm, n), lambda i: (jnp.maximum(i - 1, 0), 0))
    pos = pl.BlockSpec((tm, LANES), lambda i: (jnp.maximum(i - 1, 0) % nb, 0))
    specs, ops = _layered(params)
    p_tail = LANES + dmem
    c_in, c_ops, c_out, c_shapes = _cast_slabs(cast_next, t // tm + 1)
    out = pl.pallas_call(
        _ffn_mla_in_body,
        grid=(t // tm + 1,),
        in_specs=[cur(d)] + specs + [pos, pos] + c_in,
        out_specs=[cur(d), prev(dq), prev(dq), pl.BlockSpec((dvt, tm), lambda i: (0, jnp.minimum(i, last))),
                   prev(dmem)] + c_out,
        out_shape=[jax.ShapeDtypeStruct((t, d), F32), jax.ShapeDtypeStruct((t, dq), BF16),
                   jax.ShapeDtypeStruct((t, dq), BF16), jax.ShapeDtypeStruct((dvt, t), BF16),
                   jax.ShapeDtypeStruct((t, dmem), BF16)] + c_shapes,
        scratch_shapes=[pltpu.VMEM((tm, p_tail), F32), pltpu.VMEM((tm, dq), F32),
                        pltpu.VMEM((tm, MLA_HEADS * MLA_NOPE), F32)],
        compiler_params=_params("arbitrary"),
        name="ffn_mla_in",
    )(x, *ops, cos, sin, *c_ops)
    return out[:5], out[5:]


def _flash_body(q_ref, k_ref, vt_ref, o_ref, *, tq, tk, rc):
    qi = pl.program_id(2)
    n_diag = tq // tk
    n_chain = tq // rc
    dv = vt_ref.shape[0]
    ones = jnp.ones((ONES_ROWS, tk), BF16)

    def run_blocks(j0, state, diagonal):
        items = []
        for d in range(n_diag):
            for c in range(n_chain):
                lo_q, hi_q = c * rc, (c + 1) * rc - 1
                lo_k, hi_k = d * tk, (d + 1) * tk - 1
                if diagonal and lo_k > hi_q:
                    continue
                masked = diagonal and hi_k > lo_q
                assert not masked or lo_k <= lo_q
                items.append((d, c, masked))
        blocks = {}

        def key_block(d):
            if d not in blocks:
                k0 = pl.multiple_of((j0 + d) * tk, tk)
                blocks[d] = (k_ref[pl.ds(k0, tk), :],
                             jnp.concatenate([vt_ref[:, pl.ds(k0, tk)], ones], axis=0))
            return blocks[d]

        def scores(item):
            d, c, _ = item
            return _dot_nt(key_block(d)[0], q_ref[c * rc:(c + 1) * rc, :])

        state = list(state)
        ahead = [scores(it) for it in items[:FLASH_LOOKAHEAD]]
        for i, (d, c, masked) in enumerate(items):
            if i + FLASH_LOOKAHEAD < len(items):
                ahead.append(scores(items[i + FLASH_LOOKAHEAD]))
            st = ahead.pop(0)
            if masked:
                kpos = d * tk + lax.broadcasted_iota(jnp.int32, (tk, rc), 0)
                qpos = c * rc + lax.broadcasted_iota(jnp.int32, (tk, rc), 1)
                st = jnp.where(kpos <= qpos, st, NEG_BIG)
            m, acc = state[c]
            m_new = jnp.maximum(m, jnp.max(st, axis=0, keepdims=True))
            alpha = jnp.exp2(m - m_new)
            pt = jnp.exp2(st - m_new).astype(BF16)
            state[c] = (m_new, alpha * acc + _dot(key_block(d)[1], pt))
        return tuple(state)

    init = tuple((jnp.full((1, rc), NEG_BIG, F32), jnp.zeros((dv + ONES_ROWS, rc), F32)) for _ in range(n_chain))
    state = lax.fori_loop(0, qi, lambda t, st: run_blocks(t * n_diag, st, False), init)
    state = run_blocks(qi * n_diag, state, True)
    for c in range(n_chain):
        acc = state[c][1]
        o_ref[c * rc:(c + 1) * rc, :] = (acc[:dv, :] / acc[dv:dv + 1, :]).T.astype(o_ref.dtype)


def _flash(q, k, vt, batch, seq):
    tq = min(FLASH_TQ, seq)
    tk = min(FLASH_TK, tq)
    rc = min(FLASH_CHAIN_ROWS, tq)
    q3 = q.reshape(batch, seq, -1)
    k3 = k.reshape(batch, seq, -1)
    out = pl.pallas_call(
        functools.partial(_flash_body, tq=tq, tk=tk, rc=rc),
        grid=(batch, MLA_HEADS, seq // tq),
        in_specs=[pl.BlockSpec((None, tq, MLA_QK_PAD), lambda b, h, i: (b, i, h)),
                  pl.BlockSpec((None, seq, MLA_QK_PAD), lambda b, h, i: (b, 0, h)),
                  pl.BlockSpec((MLA_DV, seq), lambda b, h, i: (h, b))],
        out_specs=pl.BlockSpec((None, tq, MLA_DV), lambda b, h, i: (b, i, h)),
        out_shape=jax.ShapeDtypeStruct((batch, seq, MLA_HEADS * MLA_DV), BF16),
        compiler_params=_params("parallel", "parallel", "arbitrary"),
        name="mla_flash",
    )(q3, k3, vt)
    return out.reshape(batch * seq, -1)


def _memkv_body(mem_ref, g_ref, w_ref, kg_ref, k_out, v_out):
    m = _rms(mem_ref[...], g_ref[...]).astype(BF16)
    kv = _dot(m, w_ref[...])
    dm = MEM_HEADS * MEM_DH
    heads = [_rms(kv[:, h * MEM_DH:(h + 1) * MEM_DH], kg_ref[...]) for h in range(MEM_HEADS)]
    k_out[...] = jnp.concatenate(heads, axis=-1).astype(BF16)
    v_out[...] = kv[:, dm:].astype(BF16)


def _memkv(mem, params):
    b, m, d = mem.shape
    specs, ops = _layered(params)
    dm = MEM_HEADS * MEM_DH
    blk = pl.BlockSpec((None, m, dm), lambda i: (i, 0, 0))
    return pl.pallas_call(
        _memkv_body,
        grid=(b,),
        in_specs=[pl.BlockSpec((None, m, d), lambda i: (i, 0, 0))] + specs,
        out_specs=[blk, blk],
        out_shape=[jax.ShapeDtypeStruct((b, m, dm), BF16)] * 2,
        compiler_params=_params("parallel"),
        name="mem_kv",
    )(mem, *ops)


def _out_body(x_ref, o_ref, qm_ref, km_ref, vm_ref, wo_ref, out_ref):
    qm = qm_ref[...]
    om = jnp.concatenate([_mem_head(qm, km_ref, vm_ref, h) for h in range(MEM_HEADS)], axis=-1)
    dmix = o_ref.shape[-1]
    y = _dot(o_ref[...], wo_ref[:dmix, :]) + _dot(om, wo_ref[dmix:, :])
    out_ref[...] = x_ref[...] + y


def _out_proj(x, o, q_mem, k_mem, v_mem, w_out, seq):
    t, d = x.shape
    tm = min(WIDE_TILE, t, seq)
    nb = seq // tm
    row = lambda n: pl.BlockSpec((tm, n), lambda i: (i, 0))
    mem_blk = pl.BlockSpec((None,) + k_mem.shape[1:], lambda i: (i // nb, 0, 0))
    return pl.pallas_call(
        _out_body,
        grid=(t // tm,),
        in_specs=[row(d), row(o.shape[1]), row(q_mem.shape[1]), mem_blk, mem_blk, _layer_spec(*w_out)],
        out_specs=row(d),
        out_shape=jax.ShapeDtypeStruct((t, d), F32),
        compiler_params=_params("parallel"),
        name="out_proj",
    )(x, o, q_mem, k_mem, v_mem, w_out[0])


def _pad_rope(w):
    half = MLA_ROPE // 2
    z = jnp.zeros(w.shape[:-1] + (LANES // 2 - half,), w.dtype)
    return jnp.concatenate([w[..., :half], z, w[..., half:], z], axis=-1)


def _rope_tables(seq):
    inv = 1.0 / (ROPE_BASE ** (jnp.arange(0, MLA_ROPE, 2, dtype=F32) / MLA_ROPE))
    ang = jnp.arange(seq, dtype=F32)[:, None] * inv[None, :]
    cos = _pad_rope(jnp.concatenate([jnp.cos(ang)] * 2, axis=-1))
    sin = _pad_rope(jnp.concatenate([-jnp.sin(ang), jnp.sin(ang)], axis=-1))
    return cos, sin


def kernel(x, mem, ffn1_norm, ffn1_w_gate, ffn1_w_up, ffn1_w_down, ffn2_norm, ffn2_w_gate, ffn2_w_up, ffn2_w_down, mix_norm, w_out, mem_norm, mem_w_kv, memq_norm, memk_norm, gla_w_in, gla_w_alpha, gla_b_alpha, gla_out_norm, mla_w_in, mla_q_norm, mla_kv_norm, mla_w_uq, mla_w_ukv, mla_qn_norm, mla_qr_norm, mla_kn_norm, mla_kr_norm):
    batch, seq, d = x.shape
    depth = ffn1_norm.shape[0]
    t = batch * seq
    xs = x.reshape(t, d)
    cos, sin = _rope_tables(seq)
    gains = lambda a: a.astype(F32)[:, None, :]
    bf = lambda a: a.astype(BF16)

    ffn_g = [gains(ffn1_norm), gains(ffn2_norm)]
    ffn_w = [[ffn1_w_gate, ffn1_w_up, ffn1_w_down], [ffn2_w_gate, ffn2_w_up, ffn2_w_down]]
    ffn_calls = [(half, i) for i in range(depth) for half in (0, 1)]
    ffn_bf = [bf(w[:1]) for w in ffn_w[0]]
    mix_g, memq_g = gains(mix_norm), gains(memq_norm)
    side = [w.reshape(1, -1, w.shape[-1]) for w in (w_out, mem_w_kv)]

    gla_dqk = gla_w_alpha.shape[-1]
    gla_dv = d
    a_lo = 2 * gla_dqk + gla_dv
    a_hi = a_lo + GLA_GATE_RANK
    pad_rank = LANES - GLA_GATE_RANK
    gla_w = bf(jnp.concatenate([gla_w_in[..., :a_lo], gla_w_in[..., a_hi:],
                                jnp.pad(gla_w_in[..., a_lo:a_hi], ((0, 0), (0, 0), (0, pad_rank)))], axis=-1))
    gla_wa = bf(jnp.pad(gla_w_alpha, ((0, 0), (0, pad_rank), (0, 0))))
    gla_ba, gla_og = gains(gla_b_alpha), gains(gla_out_norm)

    r_lo = MLA_Q_RANK + MLA_KV_RANK
    r_hi = r_lo + MLA_ROPE
    n_mla = mla_w_in.shape[0]
    mla_w = bf(jnp.concatenate([mla_w_in[..., :r_lo], _pad_rope(mla_w_in[..., r_lo:r_hi]),
                                mla_w_in[..., r_hi:]], axis=-1))
    uq = mla_w_uq.reshape(n_mla, MLA_Q_RANK, MLA_HEADS, MLA_NOPE + MLA_ROPE)
    w_uq = bf(jnp.concatenate([uq[..., :MLA_NOPE], _pad_rope(uq[..., MLA_NOPE:])], axis=-1)
              .reshape(n_mla, MLA_Q_RANK, MLA_HEADS * MLA_QK_PAD))
    ukv = mla_w_ukv.reshape(n_mla, MLA_KV_RANK, MLA_HEADS, MLA_NOPE + MLA_DV)
    w_uk = bf(ukv[..., :MLA_NOPE].reshape(n_mla, MLA_KV_RANK, -1))
    w_uvt = bf(jnp.swapaxes(ukv[..., MLA_NOPE:].reshape(n_mla, MLA_KV_RANK, -1), 1, 2))
    qk_scale = (MLA_NOPE + MLA_ROPE) ** -0.5 * LOG2_E
    mla_qg = gains(jnp.concatenate([mla_qn_norm, _pad_rope(mla_qr_norm)], axis=-1) * qk_scale)
    mla_qn, mla_kvn = gains(mla_q_norm), gains(mla_kv_norm)
    mla_kg, mla_krg = gains(mla_kn_norm), gains(_pad_rope(mla_kr_norm))

    def ffn_args(half, i):
        n = ffn_calls.index((half, i)) + 1
        nxt = (ffn_w[ffn_calls[n][0]], ffn_calls[n][1]) if n < len(ffn_calls) else None
        return [(ffn_g[half], i)] + [(w[None] if w.ndim == 2 else w, 0) for w in ffn_bf], nxt

    for i in range(depth):
        j = i // 2
        params, nxt = ffn_args(0, i)
        if i == 0:
            xs, cast = _ffn(xs, params, (nxt[0] + side, 0))
            ffn_bf = cast[:len(nxt[0])]
            w_out_b, mem_w_b = [c.reshape(w.shape) for c, w in zip(cast[len(nxt[0]):], (w_out, mem_w_kv))]
            memkv = [gains(mem_norm), mem_w_b, gains(memk_norm)]
        elif i % 2 == 0:
            xs, ffn_bf = _ffn(xs, params, nxt)
        k_mem, v_mem = _memkv(mem, [(a, i) for a in memkv])
        if i % 2 == 0:
            xs = _gla_layer(xs, k_mem, v_mem, [(mix_g, i), (gla_w, j), (gla_wa, j), (gla_ba, j), (memq_g, i),
                                               (gla_og, j), (w_out_b, i)],
                            batch, seq, gla_dqk // GLA_HEADS, gla_dv // GLA_HEADS)
        else:
            (xs, q, k, vt, q_mem), ffn_bf = _ffn_mla_in(
                xs, params + [(mix_g, i), (mla_w, j), (mla_qn, j), (mla_kvn, j), (w_uq, j), (w_uk, j),
                              (w_uvt, j), (mla_qg, j), (mla_kg, j), (mla_krg, j), (memq_g, i)],
                cos, sin, seq, nxt)
            o = _flash(q, k, vt, batch, seq)
            xs = _out_proj(xs, o, q_mem, k_mem, v_mem, (w_out_b, i), seq)
        params, nxt = ffn_args(1, i)
        xs, ffn_bf = _ffn(xs, params, nxt)
    return xs.reshape(batch, seq, d)
```

```python
import functools

import jax
import jax.numpy as jnp
from jax import lax
from jax.experimental import pallas as pl
from jax.experimental.pallas import tpu as pltpu

F32 = jnp.float32
BF16 = jnp.bfloat16

EPS = 1e-6
GLA_HEADS = 4
GLA_GATE_RANK = 16
GLA_GATE_TAU = 16.0
GLA_CHUNK = 64
MLA_HEADS = 8
MLA_Q_RANK = 384
MLA_KV_RANK = 256
MLA_NOPE = 128
MLA_ROPE = 64
MLA_DV = 128
ROPE_BASE = 10000.0
MEM_HEADS = 4
MEM_DH = 128

LANES = 128
MLA_QK_PAD = 2 * LANES
VMEM_LIMIT_BYTES = 56 * 1024 * 1024

TOKEN_TILE = 512
WIDE_TILE = 1024
SUBTILE_ROWS = 256
FLASH_TQ = 2048
FLASH_TK = 256
FLASH_CHAIN_ROWS = 256
FLASH_LOOKAHEAD = 16
CAST_ROW_ALIGN = 16
ONES_ROWS = 16
GLA_LOOKAHEAD = 4

NEG_BIG = -1e30
LOG2_E = 1.4426950408889634


def _params(*sem):
    return pltpu.CompilerParams(dimension_semantics=sem, vmem_limit_bytes=VMEM_LIMIT_BYTES)


def _layer_spec(stacked, layer):
    nd = stacked.ndim - 1
    return pl.BlockSpec((None,) + stacked.shape[1:], lambda *_: (layer,) + (0,) * nd,
                        pipeline_mode=pl.Buffered(1))


def _layered(pairs):
    return [_layer_spec(a, l) for a, l in pairs], [a for a, _ in pairs]


def _rms(x, gain, denom=None):
    d = x.shape[-1] if denom is None else denom
    ms = jnp.sum(x * x, axis=-1, keepdims=True) * (1.0 / d)
    return x * lax.rsqrt(ms + EPS) * gain


def _dot(a, b):
    return jnp.dot(a, b, preferred_element_type=F32)


def _dot_nt(a, b):
    return lax.dot_general(a, b, (((1,), (1,)), ((), ())), preferred_element_type=F32)


def _dot_tn(a, b):
    return lax.dot_general(a, b, (((0,), (0,)), ((), ())), preferred_element_type=F32)


def _pipelined_subtiles(rows, matmuls, epilogue):
    sub = min(SUBTILE_ROWS, rows)
    slices = [slice(s * sub, (s + 1) * sub) for s in range(rows // sub)]
    staged = matmuls(slices[0])
    for s, rs in enumerate(slices):
        ready = staged
        if s + 1 < len(slices):
            staged = matmuls(slices[s + 1])
        epilogue(rs, *ready)


def _cast_slabs(cast_next, steps):
    if cast_next is None:
        return [], [], [], []
    weights, layer = cast_next
    in_specs, out_specs, out_shapes = [], [], []
    for w in weights:
        _, r, c = w.shape
        slab = next(s for s in range(CAST_ROW_ALIGN, r + 1, CAST_ROW_ALIGN) if r % s == 0 and r // s <= steps)
        last = r // slab - 1
        in_specs.append(pl.BlockSpec((None, slab, c), lambda i, last=last: (layer, jnp.minimum(i, last), 0)))
        out_specs.append(pl.BlockSpec((slab, c), lambda i, last=last: (jnp.minimum(i, last), 0)))
        out_shapes.append(jax.ShapeDtypeStruct((r, c), BF16))
    return in_specs, list(weights), out_specs, out_shapes


def _cast_side_job(src_refs, dst_refs):
    for src, dst in zip(src_refs, dst_refs):
        dst[...] = src[...].astype(BF16)


def _ffn_body(x_ref, g_ref, wg_ref, wu_ref, wd_ref, *rest):
    n_cast = (len(rest) - 1) // 2
    o_ref = rest[n_cast]

    def matmuls(rs):
        h = _rms(x_ref[rs, :], g_ref[...]).astype(BF16)
        return _dot(h, wg_ref[...]), _dot(h, wu_ref[...])

    def epilogue(rs, gate, up):
        a = (gate * jax.nn.sigmoid(gate) * up).astype(BF16)
        o_ref[rs, :] = x_ref[rs, :] + 0.5 * _dot(a, wd_ref[...])

    _pipelined_subtiles(x_ref.shape[0], matmuls, epilogue)
    _cast_side_job(rest[:n_cast], rest[n_cast + 1:])


def _ffn(x, params, cast_next=None):
    t, d = x.shape
    tm = min(WIDE_TILE, t)
    row = pl.BlockSpec((tm, d), lambda i: (i, 0))
    specs, ops = _layered(params)
    c_in, c_ops, c_out, c_shapes = _cast_slabs(cast_next, t // tm)
    out = pl.pallas_call(
        _ffn_body,
        grid=(t // tm,),
        in_specs=[row] + specs + c_in,
        out_specs=[row] + c_out,
        out_shape=[jax.ShapeDtypeStruct((t, d), F32)] + c_shapes,
        compiler_params=_params("arbitrary"),
        name="ffn",
    )(x, *ops, *c_ops)
    return out[0], out[1:]


def _mem_q(p, gain):
    heads = [_rms(p[:, h * MEM_DH:(h + 1) * MEM_DH], gain) * (MEM_DH ** -0.5) for h in range(MEM_HEADS)]
    return jnp.concatenate(heads, axis=-1).astype(BF16)


def _mem_head(qm, km_ref, vm_ref, h):
    hs = slice(h * MEM_DH, (h + 1) * MEM_DH)
    s = _dot_nt(qm[:, hs], km_ref[:, hs])
    p = jnp.exp(s - jnp.max(s, axis=-1, keepdims=True))
    l = jnp.sum(p, axis=-1, keepdims=True)
    return (_dot(p.astype(BF16), vm_ref[:, hs]) / l).astype(BF16)


def _gla_layer_body(x_ref, km_ref, vm_ref, g_ref, w_ref, wa_ref, ba_ref, mqg_ref, gn_ref, wo_ref,
                    out_ref, s_ref, o_s, *, dk, dv):
    c = GLA_CHUNK
    rows = x_ref.shape[0]
    dqk = GLA_HEADS * dk
    dvt = GLA_HEADS * dv
    dmem = MEM_HEADS * MEM_DH

    @pl.when(pl.program_id(1) == 0)
    def _():
        s_ref[...] = jnp.zeros_like(s_ref)

    ri = lax.broadcasted_iota(jnp.int32, (c, c), 0)
    ci = lax.broadcasted_iota(jnp.int32, (c, c), 1)
    causal = ci <= ri
    tril = jnp.where(causal, 1.0, 0.0).astype(BF16)
    gn = gn_ref[...]

    res = {}

    def projection():
        def col(lo, n):
            return _dot(res["h"], w_ref[:, lo:lo + n])

        def s_a():
            res["h"] = _rms(x_ref[...], g_ref[...]).astype(BF16)
            a_low = col(2 * dqk + 2 * dvt + dmem, LANES).astype(BF16)
            z = _dot(a_low, wa_ref[...]) + ba_ref[...]
            log_sig = jnp.minimum(z, 0.0) - jnp.log1p(jnp.exp(-jnp.abs(z)))
            res["la"] = log_sig * (1.0 / GLA_GATE_TAU)

        def s_q():
            res["q"] = col(0, dqk).astype(BF16)

        def s_cum():
            la = res["la"]
            hi = la.astype(BF16)
            rest = la - hi.astype(F32)
            mid = rest.astype(BF16)
            lo = (rest - mid.astype(F32)).astype(BF16)
            res["bc"] = [sum(_dot(tril, t[ic * c:(ic + 1) * c, :]) for t in (hi, mid, lo))
                         for ic in range(rows // c)]

        def s_k():
            res["k"] = col(dqk, dqk).astype(BF16)

        def s_v():
            res["v"] = col(2 * dqk, dvt).astype(BF16)

        def s_r():
            res["r"] = col(2 * dqk + dvt, dvt).astype(BF16)

        def s_m():
            res["qm"] = _mem_q(col(2 * dqk + 2 * dvt, dmem), mqg_ref[...])

        return [s_a, s_q, s_cum, s_k, s_v, s_r, s_m]

    def scan():
        items = [(ic, h) for ic in range(rows // c) for h in range(GLA_HEADS)]
        ahead = []

        def independent_part(item):
            ic, h = item
            cs = slice(ic * c, (ic + 1) * c)
            ks = slice(h * dk, (h + 1) * dk)
            bc = res["bc"][ic][:, ks]
            b_last = bc[c - 1:c, :]
            b_mid = bc[c // 2 - 1:c // 2, :]
            q = res["q"][cs, ks].astype(F32) * (dk ** -0.5)
            k = res["k"][cs, ks].astype(F32)
            v = res["v"][cs, h * dv:(h + 1) * dv]
            qe = (q * jnp.exp(bc - b_mid)).astype(BF16)
            ke = (k * jnp.exp(b_mid - bc)).astype(BF16)
            sc = _dot_nt(qe, ke)
            u = _dot_tn((k * jnp.exp(b_last - bc)).astype(BF16), v)
            qi = (q * jnp.exp(bc)).astype(BF16)
            dcol = jnp.exp(jnp.broadcast_to(b_last, (dk, dk))).T
            ahead.append((sc, u, qi, v, dcol))

        def dependent_part(item):
            ic, h = item
            sc, u, qi, v, dcol = ahead.pop(0)
            cs = slice(ic * c, (ic + 1) * c)
            vs = slice(h * dv, (h + 1) * dv)
            s_prev = s_ref[h]
            o = _dot(jnp.concatenate([qi, jnp.where(causal, sc, 0.0).astype(BF16)], axis=1),
                     jnp.concatenate([s_prev.astype(BF16), v], axis=0))
            s_ref[h] = jnp.concatenate([dcol] * (dv // dk), axis=1) * s_prev + u
            r = res["r"][cs, vs].astype(F32)
            o_s[cs, vs] = (_rms(o, gn) * (r * jax.nn.sigmoid(r))).astype(BF16)

        steps = [functools.partial(independent_part, it) for it in items[:GLA_LOOKAHEAD]]
        for i, it in enumerate(items):
            if i + GLA_LOOKAHEAD < len(items):
                steps.append(functools.partial(independent_part, items[i + GLA_LOOKAHEAD]))
            steps.append(functools.partial(dependent_part, it))
        return steps

    def output():
        heads = []
        acc = {}

        def mem_head(h):
            heads.append(_mem_head(res["qm"], km_ref, vm_ref, h))

        def mix():
            acc["y"] = _dot(o_s[...], wo_ref[:dvt, :])

        def finish():
            y = acc["y"] + _dot(jnp.concatenate(heads, axis=-1), wo_ref[dvt:, :])
            out_ref[...] = x_ref[...] + y

        return [functools.partial(mem_head, h) for h in range(MEM_HEADS)] + [mix, finish]

    for step in projection() + scan() + output():
        step()


def _gla_layer(x, k_mem, v_mem, params, batch, seq, dk, dv):
    t, d = x.shape
    tm = min(WIDE_TILE, seq)
    nb = seq // tm
    row = pl.BlockSpec((tm, d), lambda b, i: (b * nb + i, 0))
    mem_blk = pl.BlockSpec((None,) + k_mem.shape[1:], lambda b, i: (b, 0, 0))
    specs, ops = _layered(params)
    return pl.pallas_call(
        functools.partial(_gla_layer_body, dk=dk, dv=dv),
        grid=(batch, nb),
        in_specs=[row, mem_blk, mem_blk] + specs,
        out_specs=row,
        out_shape=jax.ShapeDtypeStruct((t, d), F32),
        scratch_shapes=[pltpu.VMEM((GLA_HEADS, dk, dv), F32), pltpu.VMEM((tm, GLA_HEADS * dv), BF16)],
        compiler_params=_params("parallel", "arbitrary"),
        name="gla_layer",
    )(x, k_mem, v_mem, *ops)


def _rope(x, cos, sin):
    return x * cos + pltpu.roll(x, LANES // 2, axis=1) * sin


def _ffn_mla_in_body(x_ref, fg_ref, wg_ref, wu_ref, wd_ref,
                     g_ref, w_ref, qn_ref, kvn_ref, wuq_ref, wuk_ref, wuvt_ref, qg_ref, kg_ref, krg_ref, mqg_ref,
                     cos_ref, sin_ref, *rest):
    n_cast = (len(rest) - 8) // 2
    x_out, q_out, k_out, vt_out, qm_out = rest[n_cast:n_cast + 5]
    pt_s, q_s, kn_s = rest[-3:]
    o1 = MLA_Q_RANK
    o2 = o1 + MLA_KV_RANK
    qg = qg_ref[...]

    def ffn_in(rs):
        h = _rms(x_ref[rs, :], fg_ref[...]).astype(BF16)
        return _dot(h, wg_ref[...]), _dot(h, wu_ref[...])

    def ffn_out(rs, gate, up):
        a = (gate * jax.nn.sigmoid(gate) * up).astype(BF16)
        xn = x_ref[rs, :] + 0.5 * _dot(a, wd_ref[...])
        x_out[rs, :] = xn
        return xn

    def matmuls(xn):
        h = _rms(xn, g_ref[...]).astype(BF16)
        p = _dot(h, w_ref[...])
        cq = _rms(p[:, :o1], qn_ref[...]).astype(BF16)
        ckv = _rms(p[:, o1:o2], kvn_ref[...]).astype(BF16)
        return p[:, o2:], _dot(cq, wuq_ref[...]), _dot(ckv, wuk_ref[...]), _dot_nt(wuvt_ref[...], ckv)

    def epilogue(rs, p_tail, q, kn_all):
        qm_out[rs, :] = _mem_q(p_tail[:, LANES:], mqg_ref[...])
        cos = cos_ref[rs, :]
        sin = sin_ref[rs, :]
        k_rope = _rope(_rms(p_tail[:, :LANES], krg_ref[...], denom=MLA_ROPE), cos, sin).astype(BF16)
        for hd in range(MLA_HEADS):
            b0 = hd * MLA_QK_PAD
            qn = _rms(q[:, b0:b0 + LANES], qg[:, :LANES])
            qr = _rope(_rms(q[:, b0 + LANES:b0 + MLA_QK_PAD], qg[:, LANES:], denom=MLA_ROPE), cos, sin)
            q_out[rs, b0:b0 + LANES] = qn.astype(BF16)
            q_out[rs, b0 + LANES:b0 + MLA_QK_PAD] = qr.astype(BF16)
            kn = _rms(kn_all[:, hd * MLA_NOPE:(hd + 1) * MLA_NOPE], kg_ref[...])
            k_out[rs, b0:b0 + LANES] = kn.astype(BF16)
            k_out[rs, b0 + LANES:b0 + MLA_QK_PAD] = k_rope

    rows = x_ref.shape[0]
    sub = min(SUBTILE_ROWS, rows)
    slices = [slice(s * sub, (s + 1) * sub) for s in range(rows // sub)]

    @pl.when(pl.program_id(0) == 0)
    def _():
        pt_s[...] = jnp.zeros_like(pt_s)
        q_s[...] = jnp.zeros_like(q_s)
        kn_s[...] = jnp.zeros_like(kn_s)

    hidden = [ffn_in(slices[0])]
    for rs in slices:
        epilogue(rs, pt_s[rs, :], q_s[rs, :], kn_s[rs, :])
    hidden += [ffn_in(rs) for rs in slices[1:]]
    for rs, gate_up in zip(slices, hidden):
        p_tail, q, kn_all, vt = matmuls(ffn_out(rs, *gate_up))
        vt_out[:, rs] = vt.astype(BF16)
        pt_s[rs, :] = p_tail
        q_s[rs, :] = q
        kn_s[rs, :] = kn_all
    _cast_side_job(rest[:n_cast], rest[n_cast + 5:-3])


def _ffn_mla_in(x, params, cos, sin, seq, cast_next=None):
    t, d = x.shape
    tm = min(TOKEN_TILE, t, seq)
    nb = seq // tm
    last = t // tm - 1
    dq = MLA_HEADS * MLA_QK_PAD
    dvt = MLA_HEADS * MLA_DV
    dmem = MEM_HEADS * MEM_DH
    cur = lambda n: pl.BlockSpec((tm, n), lambda i: (jnp.minimum(i, last), 0))
    prev = lambda n: pl.BlockSpec((tm, n), lambda i: (jnp.maximum(i - 1, 0), 0))
    pos = pl.BlockSpec((tm, LANES), lambda i: (jnp.maximum(i - 1, 0) % nb, 0))
    specs, ops = _layered(params)
    p_tail = LANES + dmem
    c_in, c_ops, c_out, c_shapes = _cast_slabs(cast_next, t // tm + 1)
    out = pl.pallas_call(
        _ffn_mla_in_body,
        grid=(t // tm + 1,),
        in_specs=[cur(d)] + specs + [pos, pos] + c_in,
        out_specs=[cur(d), prev(dq), prev(dq), pl.BlockSpec((dvt, tm), lambda i: (0, jnp.minimum(i, last))),
                   prev(dmem)] + c_out,
        out_shape=[jax.ShapeDtypeStruct((t, d), F32), jax.ShapeDtypeStruct((t, dq), BF16),
                   jax.ShapeDtypeStruct((t, dq), BF16), jax.ShapeDtypeStruct((dvt, t), BF16),
                   jax.ShapeDtypeStruct((t, dmem), BF16)] + c_shapes,
        scratch_shapes=[pltpu.VMEM((tm, p_tail), F32), pltpu.VMEM((tm, dq), F32),
                        pltpu.VMEM((tm, MLA_HEADS * MLA_NOPE), F32)],
        compiler_params=_params("arbitrary"),
        name="ffn_mla_in",
    )(x, *ops, cos, sin, *c_ops)
    return out[:5], out[5:]


def _flash_body(q_ref, k_ref, vt_ref, o_ref, *, tq, tk, rc):
    qi = pl.program_id(2)
    n_diag = tq // tk
    n_chain = tq // rc
    dv = vt_ref.shape[0]
    ones = jnp.ones((ONES_ROWS, tk), BF16)

    def run_blocks(j0, state, diagonal):
        items = []
        for d in range(n_diag):
            for c in range(n_chain):
                lo_q, hi_q = c * rc, (c + 1) * rc - 1
                lo_k, hi_k = d * tk, (d + 1) * tk - 1
                if diagonal and lo_k > hi_q:
                    continue
                masked = diagonal and hi_k > lo_q
                assert not masked or lo_k <= lo_q
                items.append((d, c, masked))
        blocks = {}

        def key_block(d):
            if d not in blocks:
                k0 = pl.multiple_of((j0 + d) * tk, tk)
                blocks[d] = (k_ref[pl.ds(k0, tk), :],
                             jnp.concatenate([vt_ref[:, pl.ds(k0, tk)], ones], axis=0))
            return blocks[d]

        def scores(item):
            d, c, _ = item
            return _dot_nt(key_block(d)[0], q_ref[c * rc:(c + 1) * rc, :])

        state = list(state)
        ahead = [scores(it) for it in items[:FLASH_LOOKAHEAD]]
        for i, (d, c, masked) in enumerate(items):
            if i + FLASH_LOOKAHEAD < len(items):
                ahead.append(scores(items[i + FLASH_LOOKAHEAD]))
            st = ahead.pop(0)
            if masked:
                kpos = d * tk + lax.broadcasted_iota(jnp.int32, (tk, rc), 0)
                qpos = c * rc + lax.broadcasted_iota(jnp.int32, (tk, rc), 1)
                st = jnp.where(kpos <= qpos, st, NEG_BIG)
            m, acc = state[c]
            m_new = jnp.maximum(m, jnp.max(st, axis=0, keepdims=True))
            alpha = jnp.exp2(m - m_new)
            pt = jnp.exp2(st - m_new).astype(BF16)
            state[c] = (m_new, alpha * acc + _dot(key_block(d)[1], pt))
        return tuple(state)

    init = tuple((jnp.full((1, rc), NEG_BIG, F32), jnp.zeros((dv + ONES_ROWS, rc), F32)) for _ in range(n_chain))
    state = lax.fori_loop(0, qi, lambda t, st: run_blocks(t * n_diag, st, False), init)
    state = run_blocks(qi * n_diag, state, True)
    for c in range(n_chain):
        acc = state[c][1]
        o_ref[c * rc:(c + 1) * rc, :] = (acc[:dv, :] / acc[dv:dv + 1, :]).T.astype(o_ref.dtype)


def _flash(q, k, vt, batch, seq):
    tq = min(FLASH_TQ, seq)
    tk = min(FLASH_TK, tq)
    rc = min(FLASH_CHAIN_ROWS, tq)
    q3 = q.reshape(batch, seq, -1)
    k3 = k.reshape(batch, seq, -1)
    out = pl.pallas_call(
        functools.partial(_flash_body, tq=tq, tk=tk, rc=rc),
        grid=(batch, MLA_HEADS, seq // tq),
        in_specs=[pl.BlockSpec((None, tq, MLA_QK_PAD), lambda b, h, i: (b, i, h)),
                  pl.BlockSpec((None, seq, MLA_QK_PAD), lambda b, h, i: (b, 0, h)),
                  pl.BlockSpec((MLA_DV, seq), lambda b, h, i: (h, b))],
        out_specs=pl.BlockSpec((None, tq, MLA_DV), lambda b, h, i: (b, i, h)),
        out_shape=jax.ShapeDtypeStruct((batch, seq, MLA_HEADS * MLA_DV), BF16),
        compiler_params=_params("parallel", "parallel", "arbitrary"),
        name="mla_flash",
    )(q3, k3, vt)
    return out.reshape(batch * seq, -1)


def _memkv_body(mem_ref, g_ref, w_ref, kg_ref, k_out, v_out):
    m = _rms(mem_ref[...], g_ref[...]).astype(BF16)
    kv = _dot(m, w_ref[...])
    dm = MEM_HEADS * MEM_DH
    heads = [_rms(kv[:, h * MEM_DH:(h + 1) * MEM_DH], kg_ref[...]) for h in range(MEM_HEADS)]
    k_out[...] = jnp.concatenate(heads, axis=-1).astype(BF16)
    v_out[...] = kv[:, dm:].astype(BF16)


def _memkv(mem, params):
    b, m, d = mem.shape
    specs, ops = _layered(params)
    dm = MEM_HEADS * MEM_DH
    blk = pl.BlockSpec((None, m, dm), lambda i: (i, 0, 0))
    return pl.pallas_call(
        _memkv_body,
        grid=(b,),
        in_specs=[pl.BlockSpec((None, m, d), lambda i: (i, 0, 0))] + specs,
        out_specs=[blk, blk],
        out_shape=[jax.ShapeDtypeStruct((b, m, dm), BF16)] * 2,
        compiler_params=_params("parallel"),
        name="mem_kv",
    )(mem, *ops)


def _out_body(x_ref, o_ref, qm_ref, km_ref, vm_ref, wo_ref, out_ref):
    qm = qm_ref[...]
    om = jnp.concatenate([_mem_head(qm, km_ref, vm_ref, h) for h in range(MEM_HEADS)], axis=-1)
    dmix = o_ref.shape[-1]
    y = _dot(o_ref[...], wo_ref[:dmix, :]) + _dot(om, wo_ref[dmix:, :])
    out_ref[...] = x_ref[...] + y


def _out_proj(x, o, q_mem, k_mem, v_mem, w_out, seq):
    t, d = x.shape
    tm = min(WIDE_TILE, t, seq)
    nb = seq // tm
    row = lambda n: pl.BlockSpec((tm, n), lambda i: (i, 0))
    mem_blk = pl.BlockSpec((None,) + k_mem.shape[1:], lambda i: (i // nb, 0, 0))
    return pl.pallas_call(
        _out_body,
        grid=(t // tm,),
        in_specs=[row(d), row(o.shape[1]), row(q_mem.shape[1]), mem_blk, mem_blk, _layer_spec(*w_out)],
        out_specs=row(d),
        out_shape=jax.ShapeDtypeStruct((t, d), F32),
        compiler_params=_params("parallel"),
        name="out_proj",
    )(x, o, q_mem, k_mem, v_mem, w_out[0])


def _pad_rope(w):
    half = MLA_ROPE // 2
    z = jnp.zeros(w.shape[:-1] + (LANES // 2 - half,), w.dtype)
    return jnp.concatenate([w[..., :half], z, w[..., half:], z], axis=-1)


def _rope_tables(seq):
    inv = 1.0 / (ROPE_BASE ** (jnp.arange(0, MLA_ROPE, 2, dtype=F32) / MLA_ROPE))
    ang = jnp.arange(seq, dtype=F32)[:, None] * inv[None, :]
    cos = _pad_rope(jnp.concatenate([jnp.cos(ang)] * 2, axis=-1))
    sin = _pad_rope(jnp.concatenate([-jnp.sin(ang), jnp.sin(ang)], axis=-1))
    return cos, sin


def kernel(x, mem, ffn1_norm, ffn1_w_gate, ffn1_w_up, ffn1_w_down, ffn2_norm, ffn2_w_gate, ffn2_w_up, ffn2_w_down, mix_norm, w_out, mem_norm, mem_w_kv, memq_norm, memk_norm, gla_w_in, gla_w_alpha, gla_b_alpha, gla_out_norm, mla_w_in, mla_q_norm, mla_kv_norm, mla_w_uq, mla_w_ukv, mla_qn_norm, mla_qr_norm, mla_kn_norm, mla_kr_norm):
    batch, seq, d = x.shape
    depth = ffn1_norm.shape[0]
    t = batch * seq
    xs = x.reshape(t, d)
    cos, sin = _rope_tables(seq)
    gains = lambda a: a.astype(F32)[:, None, :]
    bf = lambda a: a.astype(BF16)

    ffn_g = [gains(ffn1_norm), gains(ffn2_norm)]
    ffn_w = [[ffn1_w_gate, ffn1_w_up, ffn1_w_down], [ffn2_w_gate, ffn2_w_up, ffn2_w_down]]
    ffn_calls = [(half, i) for i in range(depth) for half in (0, 1)]
    ffn_bf = [bf(w[:1]) for w in ffn_w[0]]
    mix_g, memq_g = gains(mix_norm), gains(memq_norm)
    side = [w.reshape(1, -1, w.shape[-1]) for w in (w_out, mem_w_kv)]

    gla_dqk = gla_w_alpha.shape[-1]
    gla_dv = d
    a_lo = 2 * gla_dqk + gla_dv
    a_hi = a_lo + GLA_GATE_RANK
    pad_rank = LANES - GLA_GATE_RANK
    gla_w = bf(jnp.concatenate([gla_w_in[..., :a_lo], gla_w_in[..., a_hi:],
                                jnp.pad(gla_w_in[..., a_lo:a_hi], ((0, 0), (0, 0), (0, pad_rank)))], axis=-1))
    gla_wa = bf(jnp.pad(gla_w_alpha, ((0, 0), (0, pad_rank), (0, 0))))
    gla_ba, gla_og = gains(gla_b_alpha), gains(gla_out_norm)

    r_lo = MLA_Q_RANK + MLA_KV_RANK
    r_hi = r_lo + MLA_ROPE
    n_mla = mla_w_in.shape[0]
    mla_w = bf(jnp.concatenate([mla_w_in[..., :r_lo], _pad_rope(mla_w_in[..., r_lo:r_hi]),
                                mla_w_in[..., r_hi:]], axis=-1))
    uq = mla_w_uq.reshape(n_mla, MLA_Q_RANK, MLA_HEADS, MLA_NOPE + MLA_ROPE)
    w_uq = bf(jnp.concatenate([uq[..., :MLA_NOPE], _pad_rope(uq[..., MLA_NOPE:])], axis=-1)
              .reshape(n_mla, MLA_Q_RANK, MLA_HEADS * MLA_QK_PAD))
    ukv = mla_w_ukv.reshape(n_mla, MLA_KV_RANK, MLA_HEADS, MLA_NOPE + MLA_DV)
    w_uk = bf(ukv[..., :MLA_NOPE].reshape(n_mla, MLA_KV_RANK, -1))
    w_uvt = bf(jnp.swapaxes(ukv[..., MLA_NOPE:].reshape(n_mla, MLA_KV_RANK, -1), 1, 2))
    qk_scale = (MLA_NOPE + MLA_ROPE) ** -0.5 * LOG2_E
    mla_qg = gains(jnp.concatenate([mla_qn_norm, _pad_rope(mla_qr_norm)], axis=-1) * qk_scale)
    mla_qn, mla_kvn = gains(mla_q_norm), gains(mla_kv_norm)
    mla_kg, mla_krg = gains(mla_kn_norm), gains(_pad_rope(mla_kr_norm))

    def ffn_args(half, i):
        n = ffn_calls.index((half, i)) + 1
        nxt = (ffn_w[ffn_calls[n][0]], ffn_calls[n][1]) if n < len(ffn_calls) else None
        return [(ffn_g[half], i)] + [(w[None] if w.ndim == 2 else w, 0) for w in ffn_bf], nxt

    for i in range(depth):
        j = i // 2
        params, nxt = ffn_args(0, i)
        if i == 0:
            xs, cast = _ffn(xs, params, (nxt[0] + side, 0))
            ffn_bf = cast[:len(nxt[0])]
            w_out_b, mem_w_b = [c.reshape(w.shape) for c, w in zip(cast[len(nxt[0]):], (w_out, mem_w_kv))]
            memkv = [gains(mem_norm), mem_w_b, gains(memk_norm)]
        elif i % 2 == 0:
            xs, ffn_bf = _ffn(xs, params, nxt)
        k_mem, v_mem = _memkv(mem, [(a, i) for a in memkv])
        if i % 2 == 0:
            xs = _gla_layer(xs, k_mem, v_mem, [(mix_g, i), (gla_w, j), (gla_wa, j), (gla_ba, j), (memq_g, i),
                                               (gla_og, j), (w_out_b, i)],
                            batch, seq, gla_dqk // GLA_HEADS, gla_dv // GLA_HEADS)
        else:
            (xs, q, k, vt, q_mem), ffn_bf = _ffn_mla_in(
                xs, params + [(mix_g, i), (mla_w, j), (mla_qn, j), (mla_kvn, j), (w_uq, j), (w_uk, j),
                              (w_uvt, j), (mla_qg, j), (mla_kg, j), (mla_krg, j), (memq_g, i)],
                cos, sin, seq, nxt)
            o = _flash(q, k, vt, batch, seq)
            xs = _out_proj(xs, o, q_mem, k_mem, v_mem, (w_out_b, i), seq)
        params, nxt = ffn_args(1, i)
        xs, ffn_bf = _ffn(xs, params, nxt)
    return xs.reshape(batch, seq, d)
```

```python
import functools

import jax
import jax.numpy as jnp
from jax import lax
from jax.experimental import pallas as pl
from jax.experimental.pallas import tpu as pltpu

F32 = jnp.float32
BF16 = jnp.bfloat16

EPS = 1e-6
GLA_HEADS = 4
GLA_GATE_RANK = 16
GLA_GATE_TAU = 16.0
GLA_CHUNK = 64
MLA_HEADS = 8
MLA_Q_RANK = 384
MLA_KV_RANK = 256
MLA_NOPE = 128
MLA_ROPE = 64
MLA_DV = 128
ROPE_BASE = 10000.0
MEM_HEADS = 4
MEM_DH = 128

LANES = 128
MLA_QK_PAD = 2 * LANES
VMEM_LIMIT_BYTES = 56 * 1024 * 1024

TOKEN_TILE = 512
WIDE_TILE = 1024
SUBTILE_ROWS = 256
FLASH_TQ = 2048
FLASH_TK = 256
FLASH_CHAIN_ROWS = 256
FLASH_LOOKAHEAD = 16
CAST_FIRST_STEPS = 8
CAST_ROW_ALIGN = 16
ONES_ROWS = 16
GLA_LOOKAHEAD = 4

NEG_BIG = -1e30
LOG2_E = 1.4426950408889634


def _params(*sem):
    return pltpu.CompilerParams(dimension_semantics=sem, vmem_limit_bytes=VMEM_LIMIT_BYTES)


def _layer_spec(stacked, layer):
    nd = stacked.ndim - 1
    return pl.BlockSpec((None,) + stacked.shape[1:], lambda *_: (layer,) + (0,) * nd,
                        pipeline_mode=pl.Buffered(1))


def _layered(pairs):
    return [_layer_spec(a, l) for a, l in pairs], [a for a, _ in pairs]


def _rms(x, gain, denom=None):
    d = x.shape[-1] if denom is None else denom
    ms = jnp.sum(x * x, axis=-1, keepdims=True) * (1.0 / d)
    return x * lax.rsqrt(ms + EPS) * gain


def _dot(a, b):
    return jnp.dot(a, b, preferred_element_type=F32)


def _dot_nt(a, b):
    return lax.dot_general(a, b, (((1,), (1,)), ((), ())), preferred_element_type=F32)


def _dot_tn(a, b):
    return lax.dot_general(a, b, (((0,), (0,)), ((), ())), preferred_element_type=F32)


def _pipelined_subtiles(rows, matmuls, epilogue):
    sub = min(SUBTILE_ROWS, rows)
    slices = [slice(s * sub, (s + 1) * sub) for s in range(rows // sub)]
    staged = matmuls(slices[0])
    for s, rs in enumerate(slices):
        ready = staged
        if s + 1 < len(slices):
            staged = matmuls(slices[s + 1])
        epilogue(rs, *ready)


def _cast_slabs(cast_next, steps):
    if cast_next is None:
        return [], [], [], []
    weights, layer = cast_next
    in_specs, out_specs, out_shapes = [], [], []
    for w in weights:
        _, r, c = w.shape
        slab = next(s for s in range(CAST_ROW_ALIGN, r + 1, CAST_ROW_ALIGN) if r % s == 0 and r // s <= steps)
        last = r // slab - 1
        in_specs.append(pl.BlockSpec((None, slab, c), lambda i, last=last: (layer, jnp.minimum(i, last), 0)))
        out_specs.append(pl.BlockSpec((slab, c), lambda i, last=last: (jnp.minimum(i, last), 0)))
        out_shapes.append(jax.ShapeDtypeStruct((r, c), BF16))
    return in_specs, list(weights), out_specs, out_shapes


def _cast_side_job(src_refs, dst_refs):
    for src, dst in zip(src_refs, dst_refs):
        dst[...] = src[...].astype(BF16)


def _cast_body(*refs, n):
    _cast_side_job(refs[:n], refs[n:])


def _cast_first(weights, layer):
    c_in, c_ops, c_out, c_shapes = _cast_slabs((weights, layer), CAST_FIRST_STEPS)
    return pl.pallas_call(
        functools.partial(_cast_body, n=len(c_ops)),
        grid=(CAST_FIRST_STEPS,),
        in_specs=c_in,
        out_specs=c_out,
        out_shape=c_shapes,
        compiler_params=_params("arbitrary"),
        name="cast_first",
    )(*c_ops)


def _ffn_body(x_ref, g_ref, wg_ref, wu_ref, wd_ref, *rest):
    n_cast = (len(rest) - 1) // 2
    o_ref = rest[n_cast]

    def matmuls(rs):
        h = _rms(x_ref[rs, :], g_ref[...]).astype(BF16)
        return _dot(h, wg_ref[...]), _dot(h, wu_ref[...])

    def epilogue(rs, gate, up):
        a = (gate * jax.nn.sigmoid(gate) * up).astype(BF16)
        o_ref[rs, :] = x_ref[rs, :] + 0.5 * _dot(a, wd_ref[...])

    _pipelined_subtiles(x_ref.shape[0], matmuls, epilogue)
    _cast_side_job(rest[:n_cast], rest[n_cast + 1:])


def _ffn(x, params, cast_next=None):
    t, d = x.shape
    tm = min(WIDE_TILE, t)
    row = pl.BlockSpec((tm, d), lambda i: (i, 0))
    specs, ops = _layered(params)
    c_in, c_ops, c_out, c_shapes = _cast_slabs(cast_next, t // tm)
    out = pl.pallas_call(
        _ffn_body,
        grid=(t // tm,),
        in_specs=[row] + specs + c_in,
        out_specs=[row] + c_out,
        out_shape=[jax.ShapeDtypeStruct((t, d), F32)] + c_shapes,
        compiler_params=_params("arbitrary"),
        name="ffn",
    )(x, *ops, *c_ops)
    return out[0], out[1:]


def _mem_q(p, gain):
    heads = [_rms(p[:, h * MEM_DH:(h + 1) * MEM_DH], gain) * (MEM_DH ** -0.5) for h in range(MEM_HEADS)]
    return jnp.concatenate(heads, axis=-1).astype(BF16)


def _mem_head(qm, km_ref, vm_ref, h):
    hs = slice(h * MEM_DH, (h + 1) * MEM_DH)
    s = _dot_nt(qm[:, hs], km_ref[:, hs])
    p = jnp.exp(s - jnp.max(s, axis=-1, keepdims=True))
    l = jnp.sum(p, axis=-1, keepdims=True)
    return (_dot(p.astype(BF16), vm_ref[:, hs]) / l).astype(BF16)


def _gla_layer_body(x_ref, km_ref, vm_ref, g_ref, w_ref, wa_ref, ba_ref, mqg_ref, gn_ref, wo_ref,
                    out_ref, s_ref, o_s, *, dk, dv):
    c = GLA_CHUNK
    rows = x_ref.shape[0]
    sub = rows
    dqk = GLA_HEADS * dk
    dvt = GLA_HEADS * dv
    dmem = MEM_HEADS * MEM_DH

    @pl.when(pl.program_id(1) == 0)
    def _():
        s_ref[...] = jnp.zeros_like(s_ref)

    ri = lax.broadcasted_iota(jnp.int32, (c, c), 0)
    ci = lax.broadcasted_iota(jnp.int32, (c, c), 1)
    causal = ci <= ri
    tril = jnp.where(causal, 1.0, 0.0).astype(BF16)
    gn = gn_ref[...]

    def projection(rs, res):
        def col(lo, n):
            return _dot(res["h"], w_ref[:, lo:lo + n])

        def s_a():
            res["h"] = _rms(x_ref[rs, :], g_ref[...]).astype(BF16)
            a_low = col(2 * dqk + 2 * dvt + dmem, LANES).astype(BF16)
            z = _dot(a_low, wa_ref[...]) + ba_ref[...]
            log_sig = jnp.minimum(z, 0.0) - jnp.log1p(jnp.exp(-jnp.abs(z)))
            res["la"] = log_sig * (1.0 / GLA_GATE_TAU)

        def s_q():
            res["q"] = col(0, dqk).astype(BF16)

        def s_cum():
            la = res["la"]
            hi = la.astype(BF16)
            rest = la - hi.astype(F32)
            mid = rest.astype(BF16)
            lo = (rest - mid.astype(F32)).astype(BF16)
            res["bc"] = [sum(_dot(tril, t[ic * c:(ic + 1) * c, :]) for t in (hi, mid, lo))
                         for ic in range(sub // c)]

        def s_k():
            res["k"] = col(dqk, dqk).astype(BF16)

        def s_v():
            res["v"] = col(2 * dqk, dvt).astype(BF16)

        def s_r():
            res["r"] = col(2 * dqk + dvt, dvt).astype(BF16)

        def s_m():
            res["qm"] = _mem_q(col(2 * dqk + 2 * dvt, dmem), mqg_ref[...])

        return [s_a, s_q, s_cum, s_k, s_v, s_r, s_m]

    def scan():
        items = [(s, ic, h) for s in range(len(slices)) for ic in range(sub // c) for h in range(GLA_HEADS)]
        ahead = []

        def independent_part(item):
            s, ic, h = item
            res = results[s]
            cs = slice(ic * c, (ic + 1) * c)
            ks = slice(h * dk, (h + 1) * dk)
            bc = res["bc"][ic][:, ks]
            b_last = bc[c - 1:c, :]
            b_mid = bc[c // 2 - 1:c // 2, :]
            q = res["q"][cs, ks].astype(F32) * (dk ** -0.5)
            k = res["k"][cs, ks].astype(F32)
            v = res["v"][cs, h * dv:(h + 1) * dv]
            qe = (q * jnp.exp(bc - b_mid)).astype(BF16)
            ke = (k * jnp.exp(b_mid - bc)).astype(BF16)
            sc = _dot_nt(qe, ke)
            u = _dot_tn((k * jnp.exp(b_last - bc)).astype(BF16), v)
            qi = (q * jnp.exp(bc)).astype(BF16)
            dcol = jnp.exp(jnp.broadcast_to(b_last, (dk, dk))).T
            ahead.append((sc, u, qi, v, dcol))

        def dependent_part(item):
            s, ic, h = item
            res, r0 = results[s], s * sub
            sc, u, qi, v, dcol = ahead.pop(0)
            cs = slice(ic * c, (ic + 1) * c)
            vs = slice(h * dv, (h + 1) * dv)
            s_prev = s_ref[h]
            o = _dot(jnp.concatenate([qi, jnp.where(causal, sc, 0.0).astype(BF16)], axis=1),
                     jnp.concatenate([s_prev.astype(BF16), v], axis=0))
            s_ref[h] = jnp.concatenate([dcol] * (dv // dk), axis=1) * s_prev + u
            r = res["r"][cs, vs].astype(F32)
            o_s[r0 + ic * c:r0 + (ic + 1) * c, vs] = (_rms(o, gn) * (r * jax.nn.sigmoid(r))).astype(BF16)

        steps = [functools.partial(independent_part, it) for it in items[:GLA_LOOKAHEAD]]
        for i, it in enumerate(items):
            if i + GLA_LOOKAHEAD < len(items):
                steps.append(functools.partial(independent_part, items[i + GLA_LOOKAHEAD]))
            steps.append(functools.partial(dependent_part, it))
        return steps

    def output(rs, res):
        heads = []
        acc = {}

        def mem_head(h):
            heads.append(_mem_head(res["qm"], km_ref, vm_ref, h))

        def mix():
            acc["y"] = _dot(o_s[rs, :], wo_ref[:dvt, :])

        def finish():
            y = acc["y"] + _dot(jnp.concatenate(heads, axis=-1), wo_ref[dvt:, :])
            out_ref[rs, :] = x_ref[rs, :] + y

        return [functools.partial(mem_head, h) for h in range(MEM_HEADS)] + [mix, finish]

    slices = [slice(0, rows)]
    results = [{}]
    for step in projection(slices[0], results[0]) + scan() + output(slices[0], results[0]):
        step()


def _gla_layer(x, k_mem, v_mem, params, batch, seq, dk, dv):
    t, d = x.shape
    tm = min(WIDE_TILE, seq)
    nb = seq // tm
    row = pl.BlockSpec((tm, d), lambda b, i: (b * nb + i, 0))
    mem_blk = pl.BlockSpec((None,) + k_mem.shape[1:], lambda b, i: (b, 0, 0))
    specs, ops = _layered(params)
    return pl.pallas_call(
        functools.partial(_gla_layer_body, dk=dk, dv=dv),
        grid=(batch, nb),
        in_specs=[row, mem_blk, mem_blk] + specs,
        out_specs=row,
        out_shape=jax.ShapeDtypeStruct((t, d), F32),
        scratch_shapes=[pltpu.VMEM((GLA_HEADS, dk, dv), F32), pltpu.VMEM((tm, GLA_HEADS * dv), BF16)],
        compiler_params=_params("parallel", "arbitrary"),
        name="gla_layer",
    )(x, k_mem, v_mem, *ops)


def _rope(x, cos, sin):
    return x * cos + pltpu.roll(x, LANES // 2, axis=1) * sin


def _ffn_mla_in_body(x_ref, fg_ref, wg_ref, wu_ref, wd_ref,
                     g_ref, w_ref, qn_ref, kvn_ref, wuq_ref, wuk_ref, wuvt_ref, qg_ref, kg_ref, krg_ref, mqg_ref,
                     cos_ref, sin_ref, *rest):
    n_cast = (len(rest) - 8) // 2
    x_out, q_out, k_out, vt_out, qm_out = rest[n_cast:n_cast + 5]
    pt_s, q_s, kn_s = rest[-3:]
    o1 = MLA_Q_RANK
    o2 = o1 + MLA_KV_RANK
    qg = qg_ref[...]

    def ffn_in(rs):
        h = _rms(x_ref[rs, :], fg_ref[...]).astype(BF16)
        return _dot(h, wg_ref[...]), _dot(h, wu_ref[...])

    def ffn_out(rs, gate, up):
        a = (gate * jax.nn.sigmoid(gate) * up).astype(BF16)
        xn = x_ref[rs, :] + 0.5 * _dot(a, wd_ref[...])
        x_out[rs, :] = xn
        return xn

    def matmuls(xn):
        h = _rms(xn, g_ref[...]).astype(BF16)
        p = _dot(h, w_ref[...])
        cq = _rms(p[:, :o1], qn_ref[...]).astype(BF16)
        ckv = _rms(p[:, o1:o2], kvn_ref[...]).astype(BF16)
        return p[:, o2:], _dot(cq, wuq_ref[...]), _dot(ckv, wuk_ref[...]), _dot_nt(wuvt_ref[...], ckv)

    def epilogue(rs, p_tail, q, kn_all):
        qm_out[rs, :] = _mem_q(p_tail[:, LANES:], mqg_ref[...])
        cos = cos_ref[rs, :]
        sin = sin_ref[rs, :]
        k_rope = _rope(_rms(p_tail[:, :LANES], krg_ref[...], denom=MLA_ROPE), cos, sin).astype(BF16)
        for hd in range(MLA_HEADS):
            b0 = hd * MLA_QK_PAD
            qn = _rms(q[:, b0:b0 + LANES], qg[:, :LANES])
            qr = _rope(_rms(q[:, b0 + LANES:b0 + MLA_QK_PAD], qg[:, LANES:], denom=MLA_ROPE), cos, sin)
            q_out[rs, b0:b0 + LANES] = qn.astype(BF16)
            q_out[rs, b0 + LANES:b0 + MLA_QK_PAD] = qr.astype(BF16)
            kn = _rms(kn_all[:, hd * MLA_NOPE:(hd + 1) * MLA_NOPE], kg_ref[...])
            k_out[rs, b0:b0 + LANES] = kn.astype(BF16)
            k_out[rs, b0 + LANES:b0 + MLA_QK_PAD] = k_rope

    rows = x_ref.shape[0]
    sub = min(SUBTILE_ROWS, rows)
    slices = [slice(s * sub, (s + 1) * sub) for s in range(rows // sub)]

    @pl.when(pl.program_id(0) == 0)
    def _():
        pt_s[...] = jnp.zeros_like(pt_s)
        q_s[...] = jnp.zeros_like(q_s)
        kn_s[...] = jnp.zeros_like(kn_s)

    hidden = [ffn_in(slices[0])]
    for rs in slices:
        epilogue(rs, pt_s[rs, :], q_s[rs, :], kn_s[rs, :])
    hidden += [ffn_in(rs) for rs in slices[1:]]
    for rs, gate_up in zip(slices, hidden):
        p_tail, q, kn_all, vt = matmuls(ffn_out(rs, *gate_up))
        vt_out[:, rs] = vt.astype(BF16)
        pt_s[rs, :] = p_tail
        q_s[rs, :] = q
        kn_s[rs, :] = kn_all
    _cast_side_job(rest[:n_cast], rest[n_cast + 5:-3])


def _ffn_mla_in(x, params, cos, sin, seq, cast_next=None):
    t, d = x.shape
    tm = min(TOKEN_TILE, t, seq)
    nb = seq // tm
    last = t // tm - 1
    dq = MLA_HEADS * MLA_QK_PAD
    dvt = MLA_HEADS * MLA_DV
    dmem = MEM_HEADS * MEM_DH
    cur = lambda n: pl.BlockSpec((tm, n), lambda i: (jnp.minimum(i, last), 0))
    prev = lambda n: pl.BlockSpec((tm, n), lambda i: (jnp.maximum(i - 1, 0), 0))
    pos = pl.BlockSpec((tm, LANES), lambda i: (jnp.maximum(i - 1, 0) % nb, 0))
    specs, ops = _layered(params)
    p_tail = LANES + dmem
    c_in, c_ops, c_out, c_shapes = _cast_slabs(cast_next, t // tm + 1)
    out = pl.pallas_call(
        _ffn_mla_in_body,
        grid=(t // tm + 1,),
        in_specs=[cur(d)] + specs + [pos, pos] + c_in,
        out_specs=[cur(d), prev(dq), prev(dq), pl.BlockSpec((dvt, tm), lambda i: (0, jnp.minimum(i, last))),
                   prev(dmem)] + c_out,
        out_shape=[jax.ShapeDtypeStruct((t, d), F32), jax.ShapeDtypeStruct((t, dq), BF16),
                   jax.ShapeDtypeStruct((t, dq), BF16), jax.ShapeDtypeStruct((dvt, t), BF16),
                   jax.ShapeDtypeStruct((t, dmem), BF16)] + c_shapes,
        scratch_shapes=[pltpu.VMEM((tm, p_tail), F32), pltpu.VMEM((tm, dq), F32),
                        pltpu.VMEM((tm, MLA_HEADS * MLA_NOPE), F32)],
        compiler_params=_params("arbitrary"),
        name="ffn_mla_in",
    )(x, *ops, cos, sin, *c_ops)
    return out[:5], out[5:]


def _flash_body(q_ref, k_ref, vt_ref, o_ref, *, tq, tk, rc):
    qi = pl.program_id(2)
    n_diag = tq // tk
    n_chain = tq // rc
    dv = vt_ref.shape[0]
    ones = jnp.ones((ONES_ROWS, tk), BF16)

    def run_blocks(j0, state, diagonal):
        items = []
        for d in range(n_diag):
            for c in range(n_chain):
                lo_q, hi_q = c * rc, (c + 1) * rc - 1
                lo_k, hi_k = d * tk, (d + 1) * tk - 1
                if diagonal and lo_k > hi_q:
                    continue
                masked = diagonal and hi_k > lo_q
                assert not masked or lo_k <= lo_q
                items.append((d, c, masked))
        blocks = {}

        def key_block(d):
            if d not in blocks:
                k0 = pl.multiple_of((j0 + d) * tk, tk)
                blocks[d] = (k_ref[pl.ds(k0, tk), :],
                             jnp.concatenate([vt_ref[:, pl.ds(k0, tk)], ones], axis=0))
            return blocks[d]

        def scores(item):
            d, c, _ = item
            return _dot_nt(key_block(d)[0], q_ref[c * rc:(c + 1) * rc, :])

        state = list(state)
        ahead = [scores(it) for it in items[:FLASH_LOOKAHEAD]]
        for i, (d, c, masked) in enumerate(items):
            if i + FLASH_LOOKAHEAD < len(items):
                ahead.append(scores(items[i + FLASH_LOOKAHEAD]))
            st = ahead.pop(0)
            if masked:
                kpos = d * tk + lax.broadcasted_iota(jnp.int32, (tk, rc), 0)
                qpos = c * rc + lax.broadcasted_iota(jnp.int32, (tk, rc), 1)
                st = jnp.where(kpos <= qpos, st, NEG_BIG)
            m, acc = state[c]
            m_new = jnp.maximum(m, jnp.max(st, axis=0, keepdims=True))
            alpha = jnp.exp2(m - m_new)
            pt = jnp.exp2(st - m_new).astype(BF16)
            state[c] = (m_new, alpha * acc + _dot(key_block(d)[1], pt))
        return tuple(state)

    init = tuple((jnp.full((1, rc), NEG_BIG, F32), jnp.zeros((dv + ONES_ROWS, rc), F32)) for _ in range(n_chain))
    state = lax.fori_loop(0, qi, lambda t, st: run_blocks(t * n_diag, st, False), init)
    state = run_blocks(qi * n_diag, state, True)
    for c in range(n_chain):
        acc = state[c][1]
        o_ref[c * rc:(c + 1) * rc, :] = (acc[:dv, :] / acc[dv:dv + 1, :]).T.astype(o_ref.dtype)


def _flash(q, k, vt, batch, seq):
    tq = min(FLASH_TQ, seq)
    tk = min(FLASH_TK, tq)
    rc = min(FLASH_CHAIN_ROWS, tq)
    q3 = q.reshape(batch, seq, -1)
    k3 = k.reshape(batch, seq, -1)
    out = pl.pallas_call(
        functools.partial(_flash_body, tq=tq, tk=tk, rc=rc),
        grid=(batch, MLA_HEADS, seq // tq),
        in_specs=[pl.BlockSpec((None, tq, MLA_QK_PAD), lambda b, h, i: (b, i, h)),
                  pl.BlockSpec((None, seq, MLA_QK_PAD), lambda b, h, i: (b, 0, h)),
                  pl.BlockSpec((MLA_DV, seq), lambda b, h, i: (h, b))],
        out_specs=pl.BlockSpec((None, tq, MLA_DV), lambda b, h, i: (b, i, h)),
        out_shape=jax.ShapeDtypeStruct((batch, seq, MLA_HEADS * MLA_DV), BF16),
        compiler_params=_params("parallel", "parallel", "arbitrary"),
        name="mla_flash",
    )(q3, k3, vt)
    return out.reshape(batch * seq, -1)


def _memkv_body(mem_ref, g_ref, w_ref, kg_ref, k_out, v_out):
    m = _rms(mem_ref[...], g_ref[...]).astype(BF16)
    kv = _dot(m, w_ref[...])
    dm = MEM_HEADS * MEM_DH
    heads = [_rms(kv[:, h * MEM_DH:(h + 1) * MEM_DH], kg_ref[...]) for h in range(MEM_HEADS)]
    k_out[...] = jnp.concatenate(heads, axis=-1).astype(BF16)
    v_out[...] = kv[:, dm:].astype(BF16)


def _memkv(mem, params):
    b, m, d = mem.shape
    specs, ops = _layered(params)
    dm = MEM_HEADS * MEM_DH
    blk = pl.BlockSpec((None, m, dm), lambda i: (i, 0, 0))
    return pl.pallas_call(
        _memkv_body,
        grid=(b,),
        in_specs=[pl.BlockSpec((None, m, d), lambda i: (i, 0, 0))] + specs,
        out_specs=[blk, blk],
        out_shape=[jax.ShapeDtypeStruct((b, m, dm), BF16)] * 2,
        compiler_params=_params("parallel"),
        name="mem_kv",
    )(mem, *ops)


def _out_body(x_ref, o_ref, qm_ref, km_ref, vm_ref, wo_ref, out_ref):
    qm = qm_ref[...]
    om = jnp.concatenate([_mem_head(qm, km_ref, vm_ref, h) for h in range(MEM_HEADS)], axis=-1)
    dmix = o_ref.shape[-1]
    y = _dot(o_ref[...], wo_ref[:dmix, :]) + _dot(om, wo_ref[dmix:, :])
    out_ref[...] = x_ref[...] + y


def _out_proj(x, o, q_mem, k_mem, v_mem, w_out, seq):
    t, d = x.shape
    tm = min(WIDE_TILE, t, seq)
    nb = seq // tm
    row = lambda n: pl.BlockSpec((tm, n), lambda i: (i, 0))
    mem_blk = pl.BlockSpec((None,) + k_mem.shape[1:], lambda i: (i // nb, 0, 0))
    return pl.pallas_call(
        _out_body,
        grid=(t // tm,),
        in_specs=[row(d), row(o.shape[1]), row(q_mem.shape[1]), mem_blk, mem_blk, _layer_spec(*w_out)],
        out_specs=row(d),
        out_shape=jax.ShapeDtypeStruct((t, d), F32),
        compiler_params=_params("parallel"),
        name="out_proj",
    )(x, o, q_mem, k_mem, v_mem, w_out[0])


def _pad_rope(w):
    half = MLA_ROPE // 2
    z = jnp.zeros(w.shape[:-1] + (LANES // 2 - half,), w.dtype)
    return jnp.concatenate([w[..., :half], z, w[..., half:], z], axis=-1)


def _rope_tables(seq):
    inv = 1.0 / (ROPE_BASE ** (jnp.arange(0, MLA_ROPE, 2, dtype=F32) / MLA_ROPE))
    ang = jnp.arange(seq, dtype=F32)[:, None] * inv[None, :]
    cos = _pad_rope(jnp.concatenate([jnp.cos(ang)] * 2, axis=-1))
    sin = _pad_rope(jnp.concatenate([-jnp.sin(ang), jnp.sin(ang)], axis=-1))
    return cos, sin


def kernel(x, mem, ffn1_norm, ffn1_w_gate, ffn1_w_up, ffn1_w_down, ffn2_norm, ffn2_w_gate, ffn2_w_up, ffn2_w_down, mix_norm, w_out, mem_norm, mem_w_kv, memq_norm, memk_norm, gla_w_in, gla_w_alpha, gla_b_alpha, gla_out_norm, mla_w_in, mla_q_norm, mla_kv_norm, mla_w_uq, mla_w_ukv, mla_qn_norm, mla_qr_norm, mla_kn_norm, mla_kr_norm):
    batch, seq, d = x.shape
    depth = ffn1_norm.shape[0]
    t = batch * seq
    xs = x.reshape(t, d)
    cos, sin = _rope_tables(seq)
    gains = lambda a: a.astype(F32)[:, None, :]
    bf = lambda a: a.astype(BF16)

    ffn_g = [gains(ffn1_norm), gains(ffn2_norm)]
    ffn_w = [[ffn1_w_gate, ffn1_w_up, ffn1_w_down], [ffn2_w_gate, ffn2_w_up, ffn2_w_down]]
    ffn_calls = [(half, i) for i in range(depth) for half in (0, 1)]
    ffn_bf = _cast_first(ffn_w[0], 0)
    mix_g, memq_g = gains(mix_norm), gains(memq_norm)
    side = [w.reshape(1, -1, w.shape[-1]) for w in (w_out, mem_w_kv)]

    gla_dqk = gla_w_alpha.shape[-1]
    gla_dv = d
    a_lo = 2 * gla_dqk + gla_dv
    a_hi = a_lo + GLA_GATE_RANK
    pad_rank = LANES - GLA_GATE_RANK
    gla_w = bf(jnp.concatenate([gla_w_in[..., :a_lo], gla_w_in[..., a_hi:],
                                jnp.pad(gla_w_in[..., a_lo:a_hi], ((0, 0), (0, 0), (0, pad_rank)))], axis=-1))
    gla_wa = bf(jnp.pad(gla_w_alpha, ((0, 0), (0, pad_rank), (0, 0))))
    gla_ba, gla_og = gains(gla_b_alpha), gains(gla_out_norm)

    r_lo = MLA_Q_RANK + MLA_KV_RANK
    r_hi = r_lo + MLA_ROPE
    n_mla = mla_w_in.shape[0]
    mla_w = bf(jnp.concatenate([mla_w_in[..., :r_lo], _pad_rope(mla_w_in[..., r_lo:r_hi]),
                                mla_w_in[..., r_hi:]], axis=-1))
    uq = mla_w_uq.reshape(n_mla, MLA_Q_RANK, MLA_HEADS, MLA_NOPE + MLA_ROPE)
    w_uq = bf(jnp.concatenate([uq[..., :MLA_NOPE], _pad_rope(uq[..., MLA_NOPE:])], axis=-1)
              .reshape(n_mla, MLA_Q_RANK, MLA_HEADS * MLA_QK_PAD))
    ukv = mla_w_ukv.reshape(n_mla, MLA_KV_RANK, MLA_HEADS, MLA_NOPE + MLA_DV)
    w_uk = bf(ukv[..., :MLA_NOPE].reshape(n_mla, MLA_KV_RANK, -1))
    w_uvt = bf(jnp.swapaxes(ukv[..., MLA_NOPE:].reshape(n_mla, MLA_KV_RANK, -1), 1, 2))
    qk_scale = (MLA_NOPE + MLA_ROPE) ** -0.5 * LOG2_E
    mla_qg = gains(jnp.concatenate([mla_qn_norm, _pad_rope(mla_qr_norm)], axis=-1) * qk_scale)
    mla_qn, mla_kvn = gains(mla_q_norm), gains(mla_kv_norm)
    mla_kg, mla_krg = gains(mla_kn_norm), gains(_pad_rope(mla_kr_norm))

    def ffn_args(half, i):
        n = ffn_calls.index((half, i)) + 1
        nxt = (ffn_w[ffn_calls[n][0]], ffn_calls[n][1]) if n < len(ffn_calls) else None
        return [(ffn_g[half], i)] + [(w[None] if w.ndim == 2 else w, 0) for w in ffn_bf], nxt

    for i in range(depth):
        j = i // 2
        params, nxt = ffn_args(0, i)
        if i == 0:
            xs, cast = _ffn(xs, params, (nxt[0] + side, 0))
            ffn_bf = cast[:len(nxt[0])]
            w_out_b, mem_w_b = [c.reshape(w.shape) for c, w in zip(cast[len(nxt[0]):], (w_out, mem_w_kv))]
            memkv = [gains(mem_norm), mem_w_b, gains(memk_norm)]
        elif i % 2 == 0:
            xs, ffn_bf = _ffn(xs, params, nxt)
        k_mem, v_mem = _memkv(mem, [(a, i) for a in memkv])
        if i % 2 == 0:
            xs = _gla_layer(xs, k_mem, v_mem, [(mix_g, i), (gla_w, j), (gla_wa, j), (gla_ba, j), (memq_g, i),
                                               (gla_og, j), (w_out_b, i)],
                            batch, seq, gla_dqk // GLA_HEADS, gla_dv // GLA_HEADS)
        else:
            (xs, q, k, vt, q_mem), ffn_bf = _ffn_mla_in(
                xs, params + [(mix_g, i), (mla_w, j), (mla_qn, j), (mla_kvn, j), (w_uq, j), (w_uk, j),
                              (w_uvt, j), (mla_qg, j), (mla_kg, j), (mla_krg, j), (memq_g, i)],
                cos, sin, seq, nxt)
            o = _flash(q, k, vt, batch, seq)
            xs = _out_proj(xs, o, q_mem, k_mem, v_mem, (w_out_b, i), seq)
        params, nxt = ffn_args(1, i)
        xs, ffn_bf = _ffn(xs, params, nxt)
    return xs.reshape(batch, seq, d)
```
